```python
import math
import jax, jax.numpy as jnp
from jax import lax
import numpy as np

D_MODEL = 1024
BATCH = 16
SEQ = 2048
DEPTH = 1
DEC_BATCH = 8
DEC_SEQ = 8192
PAST_LEN = 128

MIX_WIDTH = D_MODEL
HY_WIDTH = MIX_WIDTH // 2
ATT_WIDTH = MIX_WIDTH - HY_WIDTH
HEAD_DIM = 64
N_HEADS = ATT_WIDTH // HEAD_DIM
N_KV_HEADS = 2
GROUP = N_HEADS // N_KV_HEADS
HY_COLS = 3 * HY_WIDTH
Q_COLS = N_HEADS * HEAD_DIM
KV_COLS = N_KV_HEADS * HEAD_DIM
IN_COLS = HY_COLS + Q_COLS + 2 * KV_COLS
SHORT_CONV = 3
FILTER_EMB = 33
FILTER_BANDS = (FILTER_EMB - 1) // 2
FILTER_HIDDEN = 64
N_DIRS = 2
DECAY_TARGET = 1e-2
FAST_DECAY_PCT = 0.3
SLOW_DECAY_PCT = 1.5
GRID_W = 64
ROPE_THETA = 10000.0
AXIS_DIM = HEAD_DIM // 2
Q_BLOCK = 128
D_FF = ((8 * D_MODEL + 3 * 256 - 1) // (3 * 256)) * 256
EPS = 1e-6

kernel_name = 'hymba_hyena_gqa_axial_encoder'


def _rmsnorm(x, g):
    x32 = x.astype(jnp.float32)
    y = x32 * lax.rsqrt(jnp.mean(x32 * x32, axis=-1, keepdims=True) + EPS)
    return (y * g.astype(jnp.float32)).astype(x.dtype)


def _hyena_filter(L, w1, b1, w2, b2, w3, freq, decay):
    f32 = jnp.float32
    t = jnp.linspace(0.0, 1.0, L, dtype=f32)[:, None]
    w = (2.0 * math.pi / L) * jnp.arange(L, dtype=f32)
    bands = jnp.linspace(1e-4, FILTER_BANDS - 1, FILTER_BANDS, dtype=f32)
    ang = w[:, None] * bands[None, :]
    z = jnp.concatenate([t, jnp.cos(ang), -jnp.sin(ang)], axis=-1)
    fr = freq.astype(f32)
    h = jnp.sin(fr * (z @ w1.astype(f32) + b1.astype(f32)))
    h = jnp.sin(fr * (h @ w2.astype(f32) + b2.astype(f32)))
    h = (h @ w3.astype(f32)).reshape(L, N_DIRS, HY_WIDTH)
    h = h * jnp.exp(-t[:, :, None] * jnp.abs(decay.astype(f32))[None])
    h_fwd, h_bwd = h[:, 0], h[:, 1]
    k = jnp.concatenate([h_fwd, jnp.zeros((1, HY_WIDTH), f32), h_bwd[:0:-1]], axis=0)
    return k / jnp.sum(jnp.abs(k), axis=0, keepdims=True)


def _hyena(u, conv_w, conv_b, filt, d_bias):
    L = u.shape[1]
    pad = SHORT_CONV // 2
    up = jnp.pad(u, ((0, 0), (pad, pad), (0, 0)))
    uc = conv_b + sum(up[:, j:j + L] * conv_w[j] for j in range(SHORT_CONV))
    x0, x1, v = jnp.split(uc, 3, axis=-1)
    z = (x1 * v).astype(jnp.float32)
    n = 2 * L
    zf = jnp.fft.rfft(z, n=n, axis=1)
    kf = jnp.fft.rfft(filt, n=n, axis=0)
    y = jnp.fft.irfft(zf * kf[None], n=n, axis=1)[:, :L]
    y = y + z * d_bias.astype(jnp.float32)
    return (x0.astype(jnp.float32) * y).astype(u.dtype)


def _axial_rope(L):
    f32 = jnp.float32
    rows = L // GRID_W
    row = jnp.repeat(jnp.arange(rows, dtype=f32), GRID_W)
    col = jnp.tile(jnp.arange(GRID_W, dtype=f32), rows)
    inv = ROPE_THETA ** (-jnp.arange(0, AXIS_DIM, 2, dtype=f32) / AXIS_DIM)
    ang = jnp.concatenate([row[:, None] * inv, col[:, None] * inv], axis=-1)
    return jnp.cos(ang), jnp.sin(ang)


def _apply_rope(x, cos, sin):
    xp = x.reshape(x.shape[:-1] + (HEAD_DIM // 2, 2))
    a, b = xp[..., 0], xp[..., 1]
    c = cos[None, :, None, :]
    s = sin[None, :, None, :]
    return jnp.stack([a * c - b * s, a * s + b * c], axis=-1).reshape(x.shape)


def _attention(q, k, v):
    B, L = q.shape[:2]
    nb = L // Q_BLOCK
    qb = q.reshape(B, nb, Q_BLOCK, N_KV_HEADS, GROUP, HEAD_DIM).transpose(1, 0, 2, 3, 4, 5)
    scale = HEAD_DIM ** -0.5

    def block(qi):
        s = jnp.einsum('bqkgd,bskd->bkgqs', qi, k, preferred_element_type=jnp.float32) * scale
        p = jax.nn.softmax(s, axis=-1).astype(v.dtype)
        return jnp.einsum('bkgqs,bskd->bqkgd', p, v)

    o = lax.map(block, qb)
    return o.transpose(1, 0, 2, 3, 4, 5).reshape(B, L, N_HEADS * HEAD_DIM)


def _layer(x, norm_mix_g, w_in, hy_conv_w, hy_conv_b, hy_f_w1, hy_f_b1, hy_f_w2, hy_f_b2,
           hy_f_w3, hy_f_freq, hy_decay, hy_d, q_norm_g, k_norm_g, hy_out_g, att_out_g,
           w_out, norm_ffn_g, w_gate, w_up, w_down):
    B, L, _ = x.shape
    f32 = jnp.float32
    h = _rmsnorm(x, norm_mix_g)
    proj = h @ w_in
    o1 = HY_COLS
    o2 = o1 + Q_COLS
    o3 = o2 + KV_COLS
    u_hy = proj[..., :o1]
    q = proj[..., o1:o2].reshape(B, L, N_HEADS, HEAD_DIM)
    k = proj[..., o2:o3].reshape(B, L, N_KV_HEADS, HEAD_DIM)
    v = proj[..., o3:].reshape(B, L, N_KV_HEADS, HEAD_DIM)
    filt = _hyena_filter(L, hy_f_w1, hy_f_b1, hy_f_w2, hy_f_b2, hy_f_w3, hy_f_freq, hy_decay)
    y_hy = _hyena(u_hy, hy_conv_w, hy_conv_b, filt, hy_d)
    cos, sin = _axial_rope(L)
    q = _apply_rope(_rmsnorm(q, q_norm_g).astype(f32), cos, sin).astype(x.dtype)
    k = _apply_rope(_rmsnorm(k, k_norm_g).astype(f32), cos, sin).astype(x.dtype)
    y_att = _attention(q, k, v)
    mixed = jnp.concatenate([_rmsnorm(y_hy, hy_out_g), _rmsnorm(y_att, att_out_g)], axis=-1) @ w_out
    x = x + mixed
    h = _rmsnorm(x, norm_ffn_g)
    x = x + (jax.nn.silu(h @ w_gate) * (h @ w_up)) @ w_down
    return x


def _trunk(x, layer_params, final_norm_g):
    for l in range(DEPTH):
        x = _layer(x, *[p[l] for p in layer_params])
    return _rmsnorm(x, final_norm_g)


def setup_inputs(seed: int = 0) -> dict:
    key = jax.random.key(seed)
    ks = jax.random.split(key, 24)
    f32 = jnp.float32

    def nrm(k, shape, scale):
        return jax.random.normal(k, shape, f32) * scale

    def gain(k, shape):
        return 1.0 + 0.02 * jax.random.normal(k, shape, f32)

    lo = abs(math.log(DECAY_TARGET)) / SLOW_DECAY_PCT
    hi = abs(math.log(DECAY_TARGET)) / FAST_DECAY_PCT
    decay_base = jnp.linspace(lo, hi, HY_WIDTH, dtype=f32)[None, None, :]
    return {
        'x_prompt': nrm(ks[0], (BATCH, SEQ, D_MODEL), 1.0),
        'x_sample': nrm(ks[1], (DEC_BATCH, DEC_SEQ, D_MODEL), 1.0),
        'norm_mix_g': gain(ks[2], (DEPTH, D_MODEL)),
        'w_in': nrm(ks[3], (DEPTH, D_MODEL, IN_COLS), D_MODEL ** -0.5),
        'hy_conv_w': nrm(ks[4], (DEPTH, SHORT_CONV, HY_COLS), SHORT_CONV ** -0.5),
        'hy_conv_b': nrm(ks[5], (DEPTH, HY_COLS), 0.02),
        'hy_f_w1': nrm(ks[6], (DEPTH, FILTER_EMB, FILTER_HIDDEN), FILTER_EMB ** -0.5),
        'hy_f_b1': nrm(ks[7], (DEPTH, FILTER_HIDDEN), 0.1),
        'hy_f_w2': nrm(ks[8], (DEPTH, FILTER_HIDDEN, FILTER_HIDDEN), FILTER_HIDDEN ** -0.5),
        'hy_f_b2': nrm(ks[9], (DEPTH, FILTER_HIDDEN), 0.1),
        'hy_f_w3': nrm(ks[10], (DEPTH, FILTER_HIDDEN, N_DIRS * HY_WIDTH), FILTER_HIDDEN ** -0.5),
        'hy_f_freq': gain(ks[11], (DEPTH, FILTER_HIDDEN)),
        'hy_decay': decay_base * (1.0 + 0.05 * jax.random.normal(ks[12], (DEPTH, N_DIRS, HY_WIDTH), f32)),
        'hy_d': nrm(ks[13], (DEPTH, HY_WIDTH), 1.0),
        'q_norm_g': gain(ks[14], (DEPTH, HEAD_DIM)),
        'k_norm_g': gain(ks[15], (DEPTH, HEAD_DIM)),
        'hy_out_g': gain(ks[16], (DEPTH, HY_WIDTH)),
        'att_out_g': gain(ks[17], (DEPTH, ATT_WIDTH)),
        'w_out': nrm(ks[18], (DEPTH, MIX_WIDTH, D_MODEL), MIX_WIDTH ** -0.5),
        'norm_ffn_g': gain(ks[19], (DEPTH, D_MODEL)),
        'w_gate': nrm(ks[20], (DEPTH, D_MODEL, D_FF), D_MODEL ** -0.5),
        'w_up': nrm(ks[21], (DEPTH, D_MODEL, D_FF), D_MODEL ** -0.5),
        'w_down': nrm(ks[22], (DEPTH, D_FF, D_MODEL), D_FF ** -0.5),
        'final_norm_g': gain(ks[23], (D_MODEL,)),
    }


def reference(x_prompt, x_sample, norm_mix_g, w_in, hy_conv_w, hy_conv_b, hy_f_w1, hy_f_b1,
              hy_f_w2, hy_f_b2, hy_f_w3, hy_f_freq, hy_decay, hy_d, q_norm_g, k_norm_g,
              hy_out_g, att_out_g, w_out, norm_ffn_g, w_gate, w_up, w_down, final_norm_g):
    layer_params = (norm_mix_g, w_in, hy_conv_w, hy_conv_b, hy_f_w1, hy_f_b1, hy_f_w2, hy_f_b2,
                    hy_f_w3, hy_f_freq, hy_decay, hy_d, q_norm_g, k_norm_g, hy_out_g, att_out_g,
                    w_out, norm_ffn_g, w_gate, w_up, w_down)
    y_prompt = _trunk(x_prompt, layer_params, final_norm_g)
    y_sample = _trunk(x_sample, layer_params, final_norm_g)
    return (y_prompt, y_sample)
```

```python
import functools
import math

import jax
import jax.numpy as jnp
import numpy as np
from jax import lax
from jax.experimental import pallas as pl
from jax.experimental.pallas import tpu as pltpu

F32 = jnp.float32
BF16 = jnp.bfloat16

D_MODEL = 1024
HY_WIDTH = 512
ATT_WIDTH = 512
HEAD_DIM = 64
N_HEADS = 8
N_KV_HEADS = 2
GROUP = N_HEADS // N_KV_HEADS
HY_COLS = 3 * HY_WIDTH
Q_COLS = N_HEADS * HEAD_DIM
KV_COLS = N_KV_HEADS * HEAD_DIM
QK_COLS = Q_COLS + KV_COLS
SHORT_CONV = 3
FILTER_EMB = 33
FILTER_BANDS = 16
FILTER_HIDDEN = 64
GRID_W = 64
ROPE_THETA = 10000.0
AXIS_DIM = HEAD_DIM // 2
D_FF = 2816
EPS = 1e-6

LANES = 128
SUBLANES = 8
MXU_DIM = 256
VMEM_LIMIT_BYTES = 56 * 1024 * 1024

TOKEN_TILE = 512
POST_TILE = 256
GATE_TILE = 2048
FILTER_TILE = 512
CONV_BLOCK = MXU_DIM
CONV_CH = 8
ATT_Q_TILE = 128
ATT_KV_TILE = 512


def _params(*sem):
    return pltpu.CompilerParams(dimension_semantics=sem, vmem_limit_bytes=VMEM_LIMIT_BYTES)


def _const_spec(shape):
    nd = len(shape)
    return pl.BlockSpec(shape, lambda *_: (0,) * nd, pipeline_mode=pl.Buffered(1))


def _inproj_kernel(x_ref, g_ref, why_ref, wqkv_ref, gqk_ref, mavg_ref, inv_ref, isrow_ref,
                   u_ref, q_ref, k_ref, v_ref, *, seq_len):
    tm = x_ref.shape[0]
    x = x_ref[...]
    ms = jnp.mean(x * x, axis=-1, keepdims=True)
    h = (x * lax.rsqrt(ms + EPS) * g_ref[...]).astype(BF16)
    u_ref[...] = jnp.dot(h, why_ref[...], preferred_element_type=F32)
    qkv = jnp.dot(h, wqkv_ref[...], preferred_element_type=F32)

    qk = qkv[:, :QK_COLS]
    sq = qk * qk
    sq_hi = sq.astype(BF16)
    sq_lo = (sq - sq_hi.astype(F32)).astype(BF16)
    msq = (jnp.dot(sq_hi, mavg_ref[...], preferred_element_type=F32)
           + jnp.dot(sq_lo, mavg_ref[...], preferred_element_type=F32))
    qkn = qk * lax.rsqrt(msq + EPS) * gqk_ref[...]

    t0 = (pl.program_id(0) * tm) % seq_len
    t = t0 + lax.broadcasted_iota(jnp.int32, (tm, 1), 0)
    row = (t // GRID_W).astype(F32)
    col = (t % GRID_W).astype(F32)
    ang = jnp.where(isrow_ref[...] > 0.5, row, col) * inv_ref[...]
    cos = jnp.cos(ang)
    sin = jnp.sin(ang)
    lane = lax.broadcasted_iota(jnp.int32, (1, LANES), 1)
    first_half = (lane % HEAD_DIM) < (HEAD_DIM // 2)
    sin_signed = jnp.where(first_half, -sin, sin)

    def rope(xg):
        partner = jnp.where(first_half, pltpu.roll(xg, LANES - HEAD_DIM // 2, 1),
                            pltpu.roll(xg, HEAD_DIM // 2, 1))
        return xg * cos + partner * sin_signed

    for j in range(Q_COLS // LANES):
        r = rope(qkn[:, j * LANES:(j + 1) * LANES]).astype(BF16)
        q_ref[2 * j] = r[:, :HEAD_DIM]
        q_ref[2 * j + 1] = r[:, HEAD_DIM:]
    r = rope(qkn[:, Q_COLS:QK_COLS]).astype(BF16)
    k_ref[0] = r[:, :HEAD_DIM]
    k_ref[1] = r[:, HEAD_DIM:]
    vv = qkv[:, QK_COLS:].astype(BF16)
    v_ref[0] = vv[:, :HEAD_DIM]
    v_ref[1] = vv[:, HEAD_DIM:]


def _inproj(x2, g, w_hy, w_qkv, gqk, mavg, inv_lane, isrow_lane, seq_len):
    n = x2.shape[0]
    tm = TOKEN_TILE
    return pl.pallas_call(
        functools.partial(_inproj_kernel, seq_len=seq_len),
        grid=(n // tm,),
        in_specs=[
            pl.BlockSpec((tm, D_MODEL), lambda i: (i, 0)),
            _const_spec((1, D_MODEL)),
            _const_spec((D_MODEL, HY_COLS)),
            _const_spec((D_MODEL, QK_COLS + KV_COLS)),
            _const_spec((1, QK_COLS)),
            _const_spec((QK_COLS, QK_COLS)),
            _const_spec((1, LANES)),
            _const_spec((1, LANES)),
        ],
        out_specs=[
            pl.BlockSpec((tm, HY_COLS), lambda i: (i, 0)),
            pl.BlockSpec((N_HEADS, tm, HEAD_DIM), lambda i: (0, i, 0)),
            pl.BlockSpec((N_KV_HEADS, tm, HEAD_DIM), lambda i: (0, i, 0)),
            pl.BlockSpec((N_KV_HEADS, tm, HEAD_DIM), lambda i: (0, i, 0)),
        ],
        out_shape=[
            jax.ShapeDtypeStruct((n, HY_COLS), F32),
            jax.ShapeDtypeStruct((N_HEADS, n, HEAD_DIM), BF16),
            jax.ShapeDtypeStruct((N_KV_HEADS, n, HEAD_DIM), BF16),
            jax.ShapeDtypeStruct((N_KV_HEADS, n, HEAD_DIM), BF16),
        ],
        compiler_params=_params("parallel"),
        name="inproj",
    )(x2, g, w_hy, w_qkv, gqk, mavg, inv_lane, isrow_lane)


def _filter_kernel(w1t_ref, w1c_ref, w1s_ref, b1_ref, w2_ref, b2_ref, fr_ref, bands_ref,
                   w3_ref, decay_ref, k_ref, ksum_ref, *, seq_len):
    tl = k_ref.shape[1]
    hp = lax.Precision.HIGHEST
    p = pl.program_id(0) * tl + lax.broadcasted_iota(jnp.int32, (1, tl), 1)
    n = jnp.abs(p - seq_len).astype(F32)
    t = n * (1.0 / (seq_len - 1))
    w = n * (2.0 * math.pi / seq_len)
    ang = bands_ref[...] * w
    pre = (w1t_ref[...] * t
           + jnp.dot(w1c_ref[...], jnp.cos(ang), precision=hp, preferred_element_type=F32)
           - jnp.dot(w1s_ref[...], jnp.sin(ang), precision=hp, preferred_element_type=F32)
           + b1_ref[...])
    fr = fr_ref[...]
    h = jnp.sin(fr * pre)
    h = jnp.sin(fr * (jnp.dot(w2_ref[...], h, precision=hp, preferred_element_type=F32)
                      + b2_ref[...]))
    h = jnp.dot(w3_ref[...], h, precision=hp, preferred_element_type=F32)
    h = h * jnp.exp(-t * jnp.abs(decay_ref[...]))
    h = jnp.where(p == 0, 0.0, h)
    k_ref[...] = h

    @pl.when(pl.program_id(0) == 0)
    def _():
        ksum_ref[...] = jnp.zeros_like(ksum_ref)

    ksum_ref[...] += jnp.broadcast_to(jnp.sum(jnp.abs(h), axis=1, keepdims=True), ksum_ref.shape)


def _hyena_filter(seq_len, w1, b1, w2, b2, w3, freq, decay):
    tl = FILTER_TILE
    n_tiles = 2 * seq_len // tl
    half = seq_len // tl
    col = lambda a: a.reshape(-1, 1)
    bands = np.linspace(1e-4, FILTER_BANDS - 1, FILTER_BANDS, dtype=np.float32).reshape(-1, 1)
    w1t = w1.T
    args = (w1t[:, :1], w1t[:, 1:1 + FILTER_BANDS], w1t[:, 1 + FILTER_BANDS:], col(b1),
            w2.T, col(b2), col(freq), jnp.asarray(bands),
            w3.T, decay.reshape(-1, 1))
    dir_map = lambda i: (jnp.where(i >= half, 0, 1), 0)
    in_specs = [_const_spec(a.shape) for a in args[:8]]
    in_specs += [pl.BlockSpec((HY_WIDTH, FILTER_HIDDEN), dir_map),
                 pl.BlockSpec((HY_WIDTH, 1), dir_map)]
    return pl.pallas_call(
        functools.partial(_filter_kernel, seq_len=seq_len),
        grid=(n_tiles,),
        in_specs=in_specs,
        out_specs=[pl.BlockSpec((HY_WIDTH, tl), lambda i: (0, i)),
                   pl.BlockSpec((HY_WIDTH, LANES), lambda i: (0, 0))],
        out_shape=[jax.ShapeDtypeStruct((HY_WIDTH, 2 * seq_len), F32),
                   jax.ShapeDtypeStruct((HY_WIDTH, LANES), F32)],
        compiler_params=_params("arbitrary"),
        name="hyena_filter",
    )(*args)


def _gate_kernel(x0m, x0p, x0n, x1m, x1p, x1n, vm, vp, vn, w0, w1, wv, b0, b1, bv,
                 zt_ref, x0_ref):
    tl = x0m.shape[1]
    l = pl.program_id(2)
    has_prev = (l > 0).astype(F32)
    has_next = (l < pl.num_programs(2) - 1).astype(F32)
    rows = lax.broadcasted_iota(jnp.int32, (tl, 1), 0)

    def conv(main_ref, prev_ref, next_ref, w_ref, b_ref):
        main = main_ref[0]
        w = w_ref[...]
        up = jnp.where(rows == 0, prev_ref[0, SUBLANES - 1:SUBLANES, :] * has_prev,
                       pltpu.roll(main, 1, 0))
        dn = jnp.where(rows == tl - 1, next_ref[0, 0:1, :] * has_next,
                       pltpu.roll(main, tl - 1, 0))
        return b_ref[...] + up * w[0:1] + main * w[1:2] + dn * w[2:3]

    x0 = conv(x0m, x0p, x0n, w0, b0)
    x1 = conv(x1m, x1p, x1n, w1, b1)
    v = conv(vm, vp, vn, wv, bv)
    x0_ref[0] = x0
    zt_ref[...] = (x1 * v).T


def _gate(u3, conv_w, conv_b):
    b, seq_len, _ = u3.shape
    tl = min(GATE_TILE, seq_len)
    nl = seq_len // tl
    ncb = HY_WIDTH // LANES
    blocks8 = tl // SUBLANES
    last8 = seq_len // SUBLANES - 1

    def part_specs(part):
        off = part * ncb
        return [
            pl.BlockSpec((1, tl, LANES), lambda bi, c, l: (bi, l, off + c)),
            pl.BlockSpec((1, SUBLANES, LANES),
                         lambda bi, c, l: (bi, jnp.maximum(l * blocks8 - 1, 0), off + c)),
            pl.BlockSpec((1, SUBLANES, LANES),
                         lambda bi, c, l: (bi, jnp.minimum((l + 1) * blocks8, last8), off + c)),
        ]

    in_specs = part_specs(0) + part_specs(1) + part_specs(2)
    in_specs += [pl.BlockSpec((SHORT_CONV, LANES), lambda bi, c, l, off=part * ncb: (0, off + c))
                 for part in range(3)]
    in_specs += [pl.BlockSpec((1, LANES), lambda bi, c, l, off=part * ncb: (0, off + c))
                 for part in range(3)]
    return pl.pallas_call(
        _gate_kernel,
        grid=(b, ncb, nl),
        in_specs=in_specs,
        out_specs=[pl.BlockSpec((LANES, tl), lambda bi, c, l: (c, bi * nl + l)),
                   pl.BlockSpec((1, tl, LANES), lambda bi, c, l: (bi, l, c))],
        out_shape=[jax.ShapeDtypeStruct((HY_WIDTH, b * seq_len), F32),
                   jax.ShapeDtypeStruct((b, seq_len, HY_WIDTH), F32)],
        compiler_params=_params("parallel", "parallel", "parallel"),
        name="hyena_gate",
    )(u3, u3, u3, u3, u3, u3, u3, u3, u3, conv_w, conv_w, conv_w, conv_b, conv_b, conv_b)


def _longconv_kernel(z_ref, k_ref, ksum_ref, d_ref, y_ref, zrows, acc):
    ncb, bsz, seq_len = z_ref.shape
    p = CONV_BLOCK
    nb = seq_len // p

    def channel(c, carry):
        zc = z_ref[c]
        for i in range(nb):
            zrows[i * bsz:(i + 1) * bsz, :] = zc[:, i * p:(i + 1) * p]
        acc[...] = jnp.zeros_like(acc)
        inv = 1.0 / ksum_ref[c][:, :1]
        for d in range(-(nb - 1), nb):
            start = seq_len + (d - 1) * p
            seg = k_ref[c, :, start:start + 2 * p] * inv
            skew = pltpu.roll(jnp.broadcast_to(seg, (p, 2 * p)), 0, 1, stride=1, stride_axis=0)
            w_d = skew[:, p:].astype(BF16)
            j0, j1 = max(0, -d), min(nb, nb - d)
            lhs = zrows[j0 * bsz:j1 * bsz, :].astype(BF16)
            acc[(j0 + d) * bsz:(j1 + d) * bsz, :] += jnp.dot(lhs, w_d, preferred_element_type=F32)
        dc = d_ref[c][:, :1]
        for i in range(nb):
            y_ref[c, :, i * p:(i + 1) * p] = (acc[i * bsz:(i + 1) * bsz, :]
                                              + zc[:, i * p:(i + 1) * p] * dc)
        return carry

    lax.fori_loop(0, ncb, channel, 0)


def _longconv(zt3, kext, ksum, d_bias):
    c, bsz, seq_len = zt3.shape
    cb = CONV_CH
    nb = seq_len // CONV_BLOCK
    return pl.pallas_call(
        _longconv_kernel,
        grid=(c // cb,),
        in_specs=[pl.BlockSpec((cb, bsz, seq_len), lambda i: (i, 0, 0)),
                  pl.BlockSpec((cb, 1, 2 * seq_len), lambda i: (i, 0, 0)),
                  pl.BlockSpec((cb, 1, LANES), lambda i: (i, 0, 0)),
                  pl.BlockSpec((cb, 1, LANES), lambda i: (i, 0, 0))],
        out_specs=pl.BlockSpec((cb, bsz, seq_len), lambda i: (i, 0, 0)),
        out_shape=jax.ShapeDtypeStruct((c, bsz, seq_len), F32),
        scratch_shapes=[pltpu.VMEM((nb * bsz, CONV_BLOCK), F32),
                        pltpu.VMEM((nb * bsz, CONV_BLOCK), F32)],
        compiler_params=_params("parallel"),
        name="hyena_longconv",
    )(zt3, kext.reshape(c, 1, 2 * seq_len), ksum.reshape(c, 1, LANES), d_bias)


def _attn_kernel(q_ref, k_ref, v_ref, o_ref, *, kv_tile):
    g, tq, hd = q_ref.shape
    seq_len = k_ref.shape[1]
    qs = q_ref[...].reshape(g * tq, hd)

    def step(c, carry):
        m, l, acc = carry
        kc = k_ref[0, pl.ds(c * kv_tile, kv_tile), :]
        vc = v_ref[0, pl.ds(c * kv_tile, kv_tile), :]
        s = lax.dot_general(qs, kc, (((1,), (1,)), ((), ())), preferred_element_type=F32)
        m_new = jnp.maximum(m, jnp.max(s, axis=-1, keepdims=True))
        alpha = jnp.exp(m - m_new)
        pr = jnp.exp(s - m_new)
        l = alpha * l + jnp.sum(pr, axis=-1, keepdims=True)
        acc = alpha * acc + jnp.dot(pr.astype(BF16), vc, preferred_element_type=F32)
        return m_new, l, acc

    init = (jnp.full((g * tq, 1), -jnp.inf, F32), jnp.zeros((g * tq, 1), F32),
            jnp.zeros((g * tq, hd), F32))
    _, l, acc = lax.fori_loop(0, seq_len // kv_tile, step, init)
    o = acc / l
    o_ref[...] = jnp.concatenate([o[h * tq:(h + 1) * tq, :] for h in range(g)], axis=-1)


def _attention(q, k, v, bsz, seq_len):
    n = bsz * seq_len
    tq = ATT_Q_TILE
    nq = seq_len // tq
    kv_tile = min(ATT_KV_TILE, seq_len)
    return pl.pallas_call(
        functools.partial(_attn_kernel, kv_tile=kv_tile),
        grid=(bsz, N_KV_HEADS, nq),
        in_specs=[pl.BlockSpec((GROUP, tq, HEAD_DIM), lambda b, h, i: (h, b * nq + i, 0)),
                  pl.BlockSpec((1, seq_len, HEAD_DIM), lambda b, h, i: (h, b, 0)),
                  pl.BlockSpec((1, seq_len, HEAD_DIM), lambda b, h, i: (h, b, 0))],
        out_specs=pl.BlockSpec((tq, GROUP * HEAD_DIM), lambda b, h, i: (b * nq + i, h)),
        out_shape=jax.ShapeDtypeStruct((n, ATT_WIDTH), F32),
        compiler_params=_params("parallel", "parallel", "parallel"),
        name="attention",
    )(q, k, v)


def _rms(x, g):
    return x * lax.rsqrt(jnp.mean(x * x, axis=-1, keepdims=True) + EPS) * g


def _post_kernel(x_ref, yt_ref, x0_ref, ya_ref, ghy_ref, gatt_ref, wout_ref, gffn_ref,
                 wg_ref, wu_ref, wd_ref, gfin_ref, o_ref):
    y_hy = x0_ref[...] * yt_ref[...].T
    mixed_in = jnp.concatenate([_rms(y_hy, ghy_ref[...]), _rms(ya_ref[...], gatt_ref[...])],
                               axis=-1).astype(BF16)
    x1 = x_ref[...] + jnp.dot(mixed_in, wout_ref[...], preferred_element_type=F32)
    h = _rms(x1, gffn_ref[...]).astype(BF16)
    gate = jnp.dot(h, wg_ref[...], preferred_element_type=F32)
    up = jnp.dot(h, wu_ref[...], preferred_element_type=F32)
    act = (gate * jax.nn.sigmoid(gate) * up).astype(BF16)
    x2 = x1 + jnp.dot(act, wd_ref[...], preferred_element_type=F32)
    o_ref[...] = _rms(x2, gfin_ref[...])


def _post(x2, yt, x0c, y_att, g_hy, g_att, w_out, g_ffn, w_gate, w_up, w_down, g_fin):
    n = x2.shape[0]
    tm = POST_TILE
    return pl.pallas_call(
        _post_kernel,
        grid=(n // tm,),
        in_specs=[
            pl.BlockSpec((tm, D_MODEL), lambda i: (i, 0)),
            pl.BlockSpec((HY_WIDTH, tm), lambda i: (0, i)),
            pl.BlockSpec((tm, HY_WIDTH), lambda i: (i, 0)),
            pl.BlockSpec((tm, ATT_WIDTH), lambda i: (i, 0)),
            _const_spec((1, HY_WIDTH)),
            _const_spec((1, ATT_WIDTH)),
            _const_spec((D_MODEL, D_MODEL)),
            _const_spec((1, D_MODEL)),
            _const_spec((D_MODEL, D_FF)),
            _const_spec((D_MODEL, D_FF)),
            _const_spec((D_FF, D_MODEL)),
            _const_spec((1, D_MODEL)),
        ],
        out_specs=pl.BlockSpec((tm, D_MODEL), lambda i: (i, 0)),
        out_shape=jax.ShapeDtypeStruct((n, D_MODEL), F32),
        compiler_params=_params("parallel"),
        name="post",
    )(x2, yt, x0c, y_att, g_hy, g_att, w_out, g_ffn, w_gate, w_up, w_down, g_fin)


def _rope_lane_constants():
    lane = np.arange(LANES)
    pair = lane % (HEAD_DIM // 2)
    inv = ROPE_THETA ** (-(2.0 * (pair % (AXIS_DIM // 2))) / AXIS_DIM)
    isrow = (pair < AXIS_DIM // 2).astype(np.float32)
    return (jnp.asarray(inv.astype(np.float32).reshape(1, LANES)),
            jnp.asarray(isrow.reshape(1, LANES)))


def _prepare_weights(norm_mix_g, w_in, hy_conv_w, hy_conv_b, hy_d, q_norm_g, k_norm_g,
                     hy_out_g, att_out_g, w_out, norm_ffn_g, w_gate, w_up, w_down, final_norm_g):
    row = lambda a: a.reshape(1, -1).astype(F32)
    perm = np.concatenate([np.arange(0, HEAD_DIM, 2), np.arange(1, HEAD_DIM, 2)])
    qk_perm = np.concatenate([h * HEAD_DIM + perm for h in range(N_HEADS + N_KV_HEADS)])
    w_hy = w_in[:, :HY_COLS].astype(BF16)
    w_qk = w_in[:, HY_COLS:HY_COLS + QK_COLS][:, qk_perm]
    w_qkv = jnp.concatenate([w_qk, w_in[:, HY_COLS + QK_COLS:]], axis=1).astype(BF16)
    scale = HEAD_DIM ** -0.5
    gqk = jnp.concatenate([jnp.tile(q_norm_g[perm] * scale, N_HEADS),
                           jnp.tile(k_norm_g[perm], N_KV_HEADS)]).reshape(1, QK_COLS)
    head_id = np.arange(QK_COLS) // HEAD_DIM
    mavg = jnp.asarray((head_id[:, None] == head_id[None, :]).astype(np.float32) / HEAD_DIM,
                       dtype=BF16)
    inv_lane, isrow_lane = _rope_lane_constants()
    d_bias = jnp.broadcast_to(hy_d.reshape(HY_WIDTH, 1, 1), (HY_WIDTH, 1, LANES))
    return dict(
        inproj=(row(norm_mix_g), w_hy, w_qkv, gqk, mavg, inv_lane, isrow_lane),
        conv_w=hy_conv_w, conv_b=row(hy_conv_b), d_bias=d_bias,
        post=(row(hy_out_g), row(att_out_g), w_out.astype(BF16), row(norm_ffn_g),
              w_gate.astype(BF16), w_up.astype(BF16), w_down.astype(BF16), row(final_norm_g)),
    )


def _trunk(x, wts, filt_params):
    bsz, seq_len, _ = x.shape
    n = bsz * seq_len
    x2 = x.reshape(n, D_MODEL)
    u, q, k, v = _inproj(x2, *wts["inproj"], seq_len)
    kext, ksum = _hyena_filter(seq_len, *filt_params)
    zt, x0c = _gate(u.reshape(bsz, seq_len, HY_COLS), wts["conv_w"], wts["conv_b"])
    yt = _longconv(zt.reshape(HY_WIDTH, bsz, seq_len), kext, ksum, wts["d_bias"])
    y_att = _attention(q, k, v, bsz, seq_len)
    out = _post(x2, yt.reshape(HY_WIDTH, n), x0c.reshape(n, HY_WIDTH), y_att, *wts["post"])
    return out.reshape(bsz, seq_len, D_MODEL)


def kernel(x_prompt, x_sample, norm_mix_g, w_in, hy_conv_w, hy_conv_b, hy_f_w1, hy_f_b1, hy_f_w2,
           hy_f_b2, hy_f_w3, hy_f_freq, hy_decay, hy_d, q_norm_g, k_norm_g, hy_out_g, att_out_g,
           w_out, norm_ffn_g, w_gate, w_up, w_down, final_norm_g):
    wts = _prepare_weights(norm_mix_g[0], w_in[0], hy_conv_w[0], hy_conv_b[0], hy_d[0],
                           q_norm_g[0], k_norm_g[0], hy_out_g[0], att_out_g[0], w_out[0],
                           norm_ffn_g[0], w_gate[0], w_up[0], w_down[0], final_norm_g)
    filt_params = (hy_f_w1[0], hy_f_b1[0], hy_f_w2[0], hy_f_b2[0], hy_f_w3[0], hy_f_freq[0],
                   hy_decay[0])
    return (_trunk(x_prompt, wts, filt_params), _trunk(x_sample, wts, filt_params))
```

```python
import functools
import math

import jax
import jax.numpy as jnp
import numpy as np
from jax import lax
from jax.experimental import pallas as pl
from jax.experimental.pallas import tpu as pltpu

F32 = jnp.float32
BF16 = jnp.bfloat16

D_MODEL = 1024
HY_WIDTH = 512
ATT_WIDTH = 512
HEAD_DIM = 64
N_HEADS = 8
N_KV_HEADS = 2
GROUP = N_HEADS // N_KV_HEADS
HY_COLS = 3 * HY_WIDTH
Q_COLS = N_HEADS * HEAD_DIM
KV_COLS = N_KV_HEADS * HEAD_DIM
QK_COLS = Q_COLS + KV_COLS
SHORT_CONV = 3
FILTER_EMB = 33
FILTER_BANDS = 16
FILTER_HIDDEN = 64
GRID_W = 64
ROPE_THETA = 10000.0
AXIS_DIM = HEAD_DIM // 2
D_FF = 2816
EPS = 1e-6

LANES = 128
SUBLANES = 8
MXU_DIM = 256
VMEM_LIMIT_BYTES = 56 * 1024 * 1024

TOKEN_TILE = 512
POST_TILE = 256
GATE_TILE = 2048
FILTER_TILE = 512
CONV_BLOCK = MXU_DIM
CONV_CH = 8
ATT_Q_TILE = 256
ATT_KV_TILE = 1024
V_EXT = HEAD_DIM + 16
SCORE_BOUND_MAX = 40.0


def _params(*sem):
    return pltpu.CompilerParams(dimension_semantics=sem, vmem_limit_bytes=VMEM_LIMIT_BYTES)


def _const_spec(shape):
    nd = len(shape)
    return pl.BlockSpec(shape, lambda *_: (0,) * nd, pipeline_mode=pl.Buffered(1))


def _inproj_kernel(x_ref, g_ref, why_ref, wqkv_ref, gqk_ref, mavg_ref, inv_ref, isrow_ref,
                   u_ref, q_ref, k_ref, v_ref, *, seq_len):
    tm = x_ref.shape[0]
    x = x_ref[...]
    ms = jnp.mean(x * x, axis=-1, keepdims=True)
    h = (x * lax.rsqrt(ms + EPS) * g_ref[...]).astype(BF16)
    u_ref[...] = jnp.dot(h, why_ref[...], preferred_element_type=F32)
    qkv = jnp.dot(h, wqkv_ref[...], preferred_element_type=F32)

    qk = qkv[:, :QK_COLS]
    sq = qk * qk
    sq_hi = sq.astype(BF16)
    sq_lo = (sq - sq_hi.astype(F32)).astype(BF16)
    msq = (jnp.dot(sq_hi, mavg_ref[...], preferred_element_type=F32)
           + jnp.dot(sq_lo, mavg_ref[...], preferred_element_type=F32))
    qkn = qk * lax.rsqrt(msq + EPS) * gqk_ref[...]

    t0 = (pl.program_id(0) * tm) % seq_len
    t = t0 + lax.broadcasted_iota(jnp.int32, (tm, 1), 0)
    row = (t // GRID_W).astype(F32)
    col = (t % GRID_W).astype(F32)
    ang = jnp.where(isrow_ref[...] > 0.5, row, col) * inv_ref[...]
    cos = jnp.cos(ang)
    sin = jnp.sin(ang)
    lane = lax.broadcasted_iota(jnp.int32, (1, LANES), 1)
    first_half = (lane % HEAD_DIM) < (HEAD_DIM // 2)
    sin_signed = jnp.where(first_half, -sin, sin)

    def rope(xg):
        partner = jnp.where(first_half, pltpu.roll(xg, LANES - HEAD_DIM // 2, 1),
                            pltpu.roll(xg, HEAD_DIM // 2, 1))
        return xg * cos + partner * sin_signed

    for j in range(Q_COLS // LANES):
        r = rope(qkn[:, j * LANES:(j + 1) * LANES])
        q_ref[j * LANES:(j + 1) * LANES, :] = r.T.astype(BF16)
    r = rope(qkn[:, Q_COLS:QK_COLS]).astype(BF16)
    k_ref[0] = r[:, :HEAD_DIM]
    k_ref[1] = r[:, HEAD_DIM:]
    vt = qkv[:, QK_COLS:].T.astype(BF16)
    ones = jnp.ones((V_EXT - HEAD_DIM, tm), BF16)
    for g in range(N_KV_HEADS):
        v_ref[g * V_EXT:g * V_EXT + HEAD_DIM, :] = vt[g * HEAD_DIM:(g + 1) * HEAD_DIM, :]
        v_ref[g * V_EXT + HEAD_DIM:(g + 1) * V_EXT, :] = ones


def _inproj(x2, g, w_hy, w_qkv, gqk, mavg, inv_lane, isrow_lane, seq_len):
    n = x2.shape[0]
    tm = TOKEN_TILE
    return pl.pallas_call(
        functools.partial(_inproj_kernel, seq_len=seq_len),
        grid=(n // tm,),
        in_specs=[
            pl.BlockSpec((tm, D_MODEL), lambda i: (i, 0)),
            _const_spec((1, D_MODEL)),
            _const_spec((D_MODEL, HY_COLS)),
            _const_spec((D_MODEL, QK_COLS + KV_COLS)),
            _const_spec((1, QK_COLS)),
            _const_spec((QK_COLS, QK_COLS)),
            _const_spec((1, LANES)),
            _const_spec((1, LANES)),
        ],
        out_specs=[
            pl.BlockSpec((tm, HY_COLS), lambda i: (i, 0)),
            pl.BlockSpec((Q_COLS, tm), lambda i: (0, i)),
            pl.BlockSpec((N_KV_HEADS, tm, HEAD_DIM), lambda i: (0, i, 0)),
            pl.BlockSpec((N_KV_HEADS * V_EXT, tm), lambda i: (0, i)),
        ],
        out_shape=[
            jax.ShapeDtypeStruct((n, HY_COLS), F32),
            jax.ShapeDtypeStruct((Q_COLS, n), BF16),
            jax.ShapeDtypeStruct((N_KV_HEADS, n, HEAD_DIM), BF16),
            jax.ShapeDtypeStruct((N_KV_HEADS * V_EXT, n), BF16),
        ],
        compiler_params=_params("parallel"),
        name="inproj",
    )(x2, g, w_hy, w_qkv, gqk, mavg, inv_lane, isrow_lane)


def _filter_kernel(w1t_ref, w1c_ref, w1s_ref, b1_ref, w2_ref, b2_ref, fr_ref, bands_ref,
                   w3_ref, decay_ref, k_ref, ksum_ref, *, seq_len):
    tl = k_ref.shape[1]
    hp = lax.Precision.HIGHEST
    p = pl.program_id(0) * tl + lax.broadcasted_iota(jnp.int32, (1, tl), 1)
    n = jnp.abs(p - seq_len).astype(F32)
    t = n * (1.0 / (seq_len - 1))
    w = n * (2.0 * math.pi / seq_len)
    ang = bands_ref[...] * w
    pre = (w1t_ref[...] * t
           + jnp.dot(w1c_ref[...], jnp.cos(ang), precision=hp, preferred_element_type=F32)
           - jnp.dot(w1s_ref[...], jnp.sin(ang), precision=hp, preferred_element_type=F32)
           + b1_ref[...])
    fr = fr_ref[...]
    h = jnp.sin(fr * pre)
    h = jnp.sin(fr * (jnp.dot(w2_ref[...], h, precision=hp, preferred_element_type=F32)
                      + b2_ref[...]))
    h = jnp.dot(w3_ref[...], h, precision=hp, preferred_element_type=F32)
    h = h * jnp.exp(-t * jnp.abs(decay_ref[...]))
    h = jnp.where(p == 0, 0.0, h)
    k_ref[...] = h

    @pl.when(pl.program_id(0) == 0)
    def _():
        ksum_ref[...] = jnp.zeros_like(ksum_ref)

    ksum_ref[...] += jnp.broadcast_to(jnp.sum(jnp.abs(h), axis=1, keepdims=True), ksum_ref.shape)


def _hyena_filter(seq_len, w1, b1, w2, b2, w3, freq, decay):
    tl = FILTER_TILE
    n_tiles = 2 * seq_len // tl
    half = seq_len // tl
    col = lambda a: a.reshape(-1, 1)
    bands = np.linspace(1e-4, FILTER_BANDS - 1, FILTER_BANDS, dtype=np.float32).reshape(-1, 1)
    w1t = w1.T
    args = (w1t[:, :1], w1t[:, 1:1 + FILTER_BANDS], w1t[:, 1 + FILTER_BANDS:], col(b1),
            w2.T, col(b2), col(freq), jnp.asarray(bands),
            w3.T, decay.reshape(-1, 1))
    dir_map = lambda i: (jnp.where(i >= half, 0, 1), 0)
    in_specs = [_const_spec(a.shape) for a in args[:8]]
    in_specs += [pl.BlockSpec((HY_WIDTH, FILTER_HIDDEN), dir_map),
                 pl.BlockSpec((HY_WIDTH, 1), dir_map)]
    return pl.pallas_call(
        functools.partial(_filter_kernel, seq_len=seq_len),
        grid=(n_tiles,),
        in_specs=in_specs,
        out_specs=[pl.BlockSpec((HY_WIDTH, tl), lambda i: (0, i)),
                   pl.BlockSpec((HY_WIDTH, LANES), lambda i: (0, 0))],
        out_shape=[jax.ShapeDtypeStruct((HY_WIDTH, 2 * seq_len), F32),
                   jax.ShapeDtypeStruct((HY_WIDTH, LANES), F32)],
        compiler_params=_params("arbitrary"),
        name="hyena_filter",
    )(*args)


def _gate_kernel(x0m, x0p, x0n, x1m, x1p, x1n, vm, vp, vn, w0, w1, wv, b0, b1, bv,
                 zt_ref, x0_ref):
    tl = x0m.shape[1]
    l = pl.program_id(2)
    has_prev = (l > 0).astype(F32)
    has_next = (l < pl.num_programs(2) - 1).astype(F32)
    rows = lax.broadcasted_iota(jnp.int32, (tl, 1), 0)

    def conv(main_ref, prev_ref, next_ref, w_ref, b_ref):
        main = main_ref[0]
        w = w_ref[...]
        up = jnp.where(rows == 0, prev_ref[0, SUBLANES - 1:SUBLANES, :] * has_prev,
                       pltpu.roll(main, 1, 0))
        dn = jnp.where(rows == tl - 1, next_ref[0, 0:1, :] * has_next,
                       pltpu.roll(main, tl - 1, 0))
        return b_ref[...] + up * w[0:1] + main * w[1:2] + dn * w[2:3]

    x0 = conv(x0m, x0p, x0n, w0, b0)
    x1 = conv(x1m, x1p, x1n, w1, b1)
    v = conv(vm, vp, vn, wv, bv)
    x0_ref[0] = x0
    zt_ref[...] = (x1 * v).T


def _gate(u3, conv_w, conv_b):
    b, seq_len, _ = u3.shape
    tl = min(GATE_TILE, seq_len)
    nl = seq_len // tl
    ncb = HY_WIDTH // LANES
    blocks8 = tl // SUBLANES
    last8 = seq_len // SUBLANES - 1

    def part_specs(part):
        off = part * ncb
        return [
            pl.BlockSpec((1, tl, LANES), lambda bi, c, l: (bi, l, off + c)),
            pl.BlockSpec((1, SUBLANES, LANES),
                         lambda bi, c, l: (bi, jnp.maximum(l * blocks8 - 1, 0), off + c)),
            pl.BlockSpec((1, SUBLANES, LANES),
                         lambda bi, c, l: (bi, jnp.minimum((l + 1) * blocks8, last8), off + c)),
        ]

    in_specs = part_specs(0) + part_specs(1) + part_specs(2)
    in_specs += [pl.BlockSpec((SHORT_CONV, LANES), lambda bi, c, l, off=part * ncb: (0, off + c))
                 for part in range(3)]
    in_specs += [pl.BlockSpec((1, LANES), lambda bi, c, l, off=part * ncb: (0, off + c))
                 for part in range(3)]
    return pl.pallas_call(
        _gate_kernel,
        grid=(b, ncb, nl),
        in_specs=in_specs,
        out_specs=[pl.BlockSpec((LANES, tl), lambda bi, c, l: (c, bi * nl + l)),
                   pl.BlockSpec((1, tl, LANES), lambda bi, c, l: (bi, l, c))],
        out_shape=[jax.ShapeDtypeStruct((HY_WIDTH, b * seq_len), F32),
                   jax.ShapeDtypeStruct((b, seq_len, HY_WIDTH), F32)],
        compiler_params=_params("parallel", "parallel", "parallel"),
        name="hyena_gate",
    )(u3, u3, u3, u3, u3, u3, u3, u3, u3, conv_w, conv_w, conv_w, conv_b, conv_b, conv_b)


def _longconv_kernel(z_ref, k_ref, ksum_ref, d_ref, y_ref, zrows, acc):
    ncb, bsz, seq_len = z_ref.shape
    p = CONV_BLOCK
    nb = seq_len // p

    def channel(c, carry):
        zc = z_ref[c]
        for i in range(nb):
            zrows[i * bsz:(i + 1) * bsz, :] = zc[:, i * p:(i + 1) * p]
        acc[...] = jnp.zeros_like(acc)
        inv = 1.0 / ksum_ref[c][:, :1]
        for d in range(-(nb - 1), nb):
            start = seq_len + (d - 1) * p
            seg = k_ref[c, :, start:start + 2 * p] * inv
            skew = pltpu.roll(jnp.broadcast_to(seg, (p, 2 * p)), 0, 1, stride=1, stride_axis=0)
            w_d = skew[:, p:].astype(BF16)
            j0, j1 = max(0, -d), min(nb, nb - d)
            lhs = zrows[j0 * bsz:j1 * bsz, :].astype(BF16)
            acc[(j0 + d) * bsz:(j1 + d) * bsz, :] += jnp.dot(lhs, w_d, preferred_element_type=F32)
        dc = d_ref[c][:, :1]
        for i in range(nb):
            y_ref[c, :, i * p:(i + 1) * p] = (acc[i * bsz:(i + 1) * bsz, :]
                                              + zc[:, i * p:(i + 1) * p] * dc)
        return carry

    lax.fori_loop(0, ncb, channel, 0)


def _longconv(zt3, kext, ksum, d_bias):
    c, bsz, seq_len = zt3.shape
    cb = CONV_CH
    nb = seq_len // CONV_BLOCK
    return pl.pallas_call(
        _longconv_kernel,
        grid=(c // cb,),
        in_specs=[pl.BlockSpec((cb, bsz, seq_len), lambda i: (i, 0, 0)),
                  pl.BlockSpec((cb, 1, 2 * seq_len), lambda i: (i, 0, 0)),
                  pl.BlockSpec((cb, 1, LANES), lambda i: (i, 0, 0)),
                  pl.BlockSpec((cb, 1, LANES), lambda i: (i, 0, 0))],
        out_specs=pl.BlockSpec((cb, bsz, seq_len), lambda i: (i, 0, 0)),
        out_shape=jax.ShapeDtypeStruct((c, bsz, seq_len), F32),
        scratch_shapes=[pltpu.VMEM((nb * bsz, CONV_BLOCK), F32),
                        pltpu.VMEM((nb * bsz, CONV_BLOCK), F32)],
        compiler_params=_params("parallel"),
        name="hyena_longconv",
    )(zt3, kext.reshape(c, 1, 2 * seq_len), ksum.reshape(c, 1, LANES), d_bias)


def _attn_kernel(q_ref, k_ref, v_ref, o_ref, m_ref, acc_ref, *, kv_tile):
    seq_len = k_ref.shape[1]
    m_ref[...] = jnp.full(m_ref.shape, -jnp.inf, F32)
    acc_ref[...] = jnp.zeros(acc_ref.shape, F32)

    def step(c, carry):
        off = pl.multiple_of(c * kv_tile, kv_tile)
        kc = k_ref[0, pl.ds(off, kv_tile), :]
        vc = v_ref[:, pl.ds(off, kv_tile)]
        for h in range(GROUP):
            qh = q_ref[h * HEAD_DIM:(h + 1) * HEAD_DIM, :]
            s = jnp.dot(kc, qh, preferred_element_type=F32)
            m_old = m_ref[h]
            m_new = jnp.maximum(m_old, jnp.max(s, axis=0, keepdims=True))
            alpha = jnp.exp(m_old - m_new)
            p = jnp.exp(s - m_new).astype(BF16)
            acc_ref[h] = alpha * acc_ref[h] + jnp.dot(vc, p, preferred_element_type=F32)
            m_ref[h] = m_new
        return carry

    lax.fori_loop(0, seq_len // kv_tile, step, 0)
    for h in range(GROUP):
        a = acc_ref[h]
        o_ref[h * HEAD_DIM:(h + 1) * HEAD_DIM, :] = a[:HEAD_DIM] / a[HEAD_DIM:HEAD_DIM + 1]


def _attn_bounded_kernel(q_ref, k_ref, v_ref, o_ref, qcat_ref, acc_ref, *, kv_tile):
    seq_len = k_ref.shape[1]
    tq = q_ref.shape[1]
    acc_ref[...] = jnp.zeros(acc_ref.shape, F32)
    for h in range(GROUP):
        qcat_ref[:, h * tq:(h + 1) * tq] = q_ref[h * HEAD_DIM:(h + 1) * HEAD_DIM, :]

    def step(c, carry):
        off = pl.multiple_of(c * kv_tile, kv_tile)
        kc = k_ref[0, pl.ds(off, kv_tile), :]
        vc = v_ref[:, pl.ds(off, kv_tile)]
        p = jnp.exp(jnp.dot(kc, qcat_ref[...], preferred_element_type=F32)).astype(BF16)
        acc_ref[...] += jnp.dot(vc, p, preferred_element_type=F32)
        return carry

    lax.fori_loop(0, seq_len // kv_tile, step, 0)
    for h in range(GROUP):
        a = acc_ref[:, h * tq:(h + 1) * tq]
        o_ref[h * HEAD_DIM:(h + 1) * HEAD_DIM, :] = a[:HEAD_DIM] / a[HEAD_DIM:HEAD_DIM + 1]


def _attention(qt, k, vt, bsz, seq_len, score_bound):
    n = bsz * seq_len
    tq = ATT_Q_TILE
    nq = seq_len // tq
    kv_tile = min(ATT_KV_TILE, seq_len)
    gw = GROUP * HEAD_DIM

    def call(body, scratch, name):
        return pl.pallas_call(
            functools.partial(body, kv_tile=kv_tile),
            grid=(bsz, N_KV_HEADS, nq),
            in_specs=[pl.BlockSpec((gw, tq), lambda b, h, i: (h, b * nq + i)),
                      pl.BlockSpec((1, seq_len, HEAD_DIM), lambda b, h, i: (h, b, 0)),
                      pl.BlockSpec((V_EXT, seq_len), lambda b, h, i: (h, b))],
            out_specs=pl.BlockSpec((gw, tq), lambda b, h, i: (h, b * nq + i)),
            out_shape=jax.ShapeDtypeStruct((ATT_WIDTH, n), F32),
            scratch_shapes=scratch,
            compiler_params=_params("parallel", "parallel", "parallel"),
            name=name,
        )

    acc = pltpu.VMEM((GROUP, V_EXT, tq), F32)
    bounded = call(_attn_bounded_kernel,
                   [pltpu.VMEM((HEAD_DIM, GROUP * tq), BF16), pltpu.VMEM((V_EXT, GROUP * tq), F32)],
                   "attention_bounded")
    online = call(_attn_kernel, [pltpu.VMEM((GROUP, 1, tq), F32), acc], "attention_online")
    return lax.cond(score_bound <= SCORE_BOUND_MAX, bounded, online, qt, k, vt)


def _rms(x, g):
    return x * lax.rsqrt(jnp.mean(x * x, axis=-1, keepdims=True) + EPS) * g


def _post_kernel(x_ref, yt_ref, x0_ref, ya_ref, ghy_ref, gatt_ref, wout_ref, gffn_ref,
                 wg_ref, wu_ref, wd_ref, gfin_ref, o_ref):
    y_hy = x0_ref[...] * yt_ref[...].T
    mixed_in = jnp.concatenate([_rms(y_hy, ghy_ref[...]), _rms(ya_ref[...].T, gatt_ref[...])],
                               axis=-1).astype(BF16)
    x1 = x_ref[...] + jnp.dot(mixed_in, wout_ref[...], preferred_element_type=F32)
    h = _rms(x1, gffn_ref[...]).astype(BF16)
    gate = jnp.dot(h, wg_ref[...], preferred_element_type=F32)
    up = jnp.dot(h, wu_ref[...], preferred_element_type=F32)
    act = (gate * jax.nn.sigmoid(gate) * up).astype(BF16)
    x2 = x1 + jnp.dot(act, wd_ref[...], preferred_element_type=F32)
    o_ref[...] = _rms(x2, gfin_ref[...])


def _post(x2, yt, x0c, y_att, g_hy, g_att, w_out, g_ffn, w_gate, w_up, w_down, g_fin):
    n = x2.shape[0]
    tm = POST_TILE
    return pl.pallas_call(
        _post_kernel,
        grid=(n // tm,),
        in_specs=[
            pl.BlockSpec((tm, D_MODEL), lambda i: (i, 0)),
            pl.BlockSpec((HY_WIDTH, tm), lambda i: (0, i)),
            pl.BlockSpec((tm, HY_WIDTH), lambda i: (i, 0)),
            pl.BlockSpec((ATT_WIDTH, tm), lambda i: (0, i)),
            _const_spec((1, HY_WIDTH)),
            _const_spec((1, ATT_WIDTH)),
            _const_spec((D_MODEL, D_MODEL)),
            _const_spec((1, D_MODEL)),
            _const_spec((D_MODEL, D_FF)),
            _const_spec((D_MODEL, D_FF)),
            _const_spec((D_FF, D_MODEL)),
            _const_spec((1, D_MODEL)),
        ],
        out_specs=pl.BlockSpec((tm, D_MODEL), lambda i: (i, 0)),
        out_shape=jax.ShapeDtypeStruct((n, D_MODEL), F32),
        compiler_params=_params("parallel"),
        name="post",
    )(x2, yt, x0c, y_att, g_hy, g_att, w_out, g_ffn, w_gate, w_up, w_down, g_fin)


def _rope_lane_constants():
    lane = np.arange(LANES)
    pair = lane % (HEAD_DIM // 2)
    inv = ROPE_THETA ** (-(2.0 * (pair % (AXIS_DIM // 2))) / AXIS_DIM)
    isrow = (pair < AXIS_DIM // 2).astype(np.float32)
    return (jnp.asarray(inv.astype(np.float32).reshape(1, LANES)),
            jnp.asarray(isrow.reshape(1, LANES)))


def _prepare_weights(norm_mix_g, w_in, hy_conv_w, hy_conv_b, hy_d, q_norm_g, k_norm_g,
                     hy_out_g, att_out_g, w_out, norm_ffn_g, w_gate, w_up, w_down, final_norm_g):
    row = lambda a: a.reshape(1, -1).astype(F32)
    perm = np.concatenate([np.arange(0, HEAD_DIM, 2), np.arange(1, HEAD_DIM, 2)])
    qk_perm = np.concatenate([h * HEAD_DIM + perm for h in range(N_HEADS + N_KV_HEADS)])
    w_hy = w_in[:, :HY_COLS].astype(BF16)
    w_qk = w_in[:, HY_COLS:HY_COLS + QK_COLS][:, qk_perm]
    w_qkv = jnp.concatenate([w_qk, w_in[:, HY_COLS + QK_COLS:]], axis=1).astype(BF16)
    scale = HEAD_DIM ** -0.5
    gqk = jnp.concatenate([jnp.tile(q_norm_g[perm] * scale, N_HEADS),
                           jnp.tile(k_norm_g[perm], N_KV_HEADS)]).reshape(1, QK_COLS)
    head_id = np.arange(QK_COLS) // HEAD_DIM
    mavg = jnp.asarray((head_id[:, None] == head_id[None, :]).astype(np.float32) / HEAD_DIM,
                       dtype=BF16)
    inv_lane, isrow_lane = _rope_lane_constants()
    d_bias = jnp.broadcast_to(hy_d.reshape(HY_WIDTH, 1, 1), (HY_WIDTH, 1, LANES))
    score_bound = (HEAD_DIM ** 0.5) * jnp.max(jnp.abs(q_norm_g)) * jnp.max(jnp.abs(k_norm_g))
    return dict(
        score_bound=score_bound,
        inproj=(row(norm_mix_g), w_hy, w_qkv, gqk, mavg, inv_lane, isrow_lane),
        conv_w=hy_conv_w, conv_b=row(hy_conv_b), d_bias=d_bias,
        post=(row(hy_out_g), row(att_out_g), w_out.astype(BF16), row(norm_ffn_g),
              w_gate.astype(BF16), w_up.astype(BF16), w_down.astype(BF16), row(final_norm_g)),
    )


def _trunk(x, wts, filt_params):
    bsz, seq_len, _ = x.shape
    n = bsz * seq_len
    x2 = x.reshape(n, D_MODEL)
    u, q, k, v = _inproj(x2, *wts["inproj"], seq_len)
    kext, ksum = _hyena_filter(seq_len, *filt_params)
    zt, x0c = _gate(u.reshape(bsz, seq_len, HY_COLS), wts["conv_w"], wts["conv_b"])
    yt = _longconv(zt.reshape(HY_WIDTH, bsz, seq_len), kext, ksum, wts["d_bias"])
    y_att = _attention(q, k, v, bsz, seq_len, wts["score_bound"])
    out = _post(x2, yt.reshape(HY_WIDTH, n), x0c.reshape(n, HY_WIDTH), y_att, *wts["post"])
    return out.reshape(bsz, seq_len, D_MODEL)


def kernel(x_prompt, x_sample, norm_mix_g, w_in, hy_conv_w, hy_conv_b, hy_f_w1, hy_f_b1, hy_f_w2,
           hy_f_b2, hy_f_w3, hy_f_freq, hy_decay, hy_d, q_norm_g, k_norm_g, hy_out_g, att_out_g,
           w_out, norm_ffn_g, w_gate, w_up, w_down, final_norm_g):
    wts = _prepare_weights(norm_mix_g[0], w_in[0], hy_conv_w[0], hy_conv_b[0], hy_d[0],
                           q_norm_g[0], k_norm_g[0], hy_out_g[0], att_out_g[0], w_out[0],
                           norm_ffn_g[0], w_gate[0], w_up[0], w_down[0], final_norm_g)
    filt_params = (hy_f_w1[0], hy_f_b1[0], hy_f_w2[0], hy_f_b2[0], hy_f_w3[0], hy_f_freq[0],
                   hy_decay[0])
    return (_trunk(x_prompt, wts, filt_params), _trunk(x_sample, wts, filt_params))
```

```python
import functools
import math

import jax
import jax.numpy as jnp
import numpy as np
from jax import lax
from jax.experimental import pallas as pl
from jax.experimental.pallas import tpu as pltpu

F32 = jnp.float32
BF16 = jnp.bfloat16

D_MODEL = 1024
HY_WIDTH = 512
ATT_WIDTH = 512
HEAD_DIM = 64
N_HEADS = 8
N_KV_HEADS = 2
GROUP = N_HEADS // N_KV_HEADS
HY_COLS = 3 * HY_WIDTH
Q_COLS = N_HEADS * HEAD_DIM
KV_COLS = N_KV_HEADS * HEAD_DIM
QK_COLS = Q_COLS + KV_COLS
SHORT_CONV = 3
FILTER_EMB = 33
FILTER_BANDS = 16
FILTER_HIDDEN = 64
GRID_W = 64
ROPE_THETA = 10000.0
AXIS_DIM = HEAD_DIM // 2
D_FF = 2816
EPS = 1e-6

LANES = 128
SUBLANES = 8
MXU_DIM = 256
VMEM_LIMIT_BYTES = 56 * 1024 * 1024

TOKEN_TILE = 512
POST_TILE = 256
HALO = 16
FILTER_TILE = 512
CONV_BLOCK = MXU_DIM
CONV_SUB = CONV_BLOCK // 2
SKEW_CHUNK = 1024
CONV_CH = 8
ATT_Q_TILE = 512
ATT_KV_TILE = 1024
V_EXT = HEAD_DIM + 16
SCORE_BOUND_MAX = 40.0


def _params(*sem):
    return pltpu.CompilerParams(dimension_semantics=sem, vmem_limit_bytes=VMEM_LIMIT_BYTES)


def _const_spec(shape):
    nd = len(shape)
    return pl.BlockSpec(shape, lambda *_: (0,) * nd, pipeline_mode=pl.Buffered(1))


def _inproj_kernel(x_ref, xp_ref, xn_ref, g_ref, why_ref, wqkv_ref, gqk_ref, mavg_ref, inv_ref,
                   isrow_ref, cw_ref, cb_ref, zt_ref, x0_ref, q_ref, k_ref, v_ref, u_scr, *,
                   seq_len):
    tm = x_ref.shape[0]
    t0 = (pl.program_id(0) * tm) % seq_len

    def normed(x):
        ms = jnp.mean(x * x, axis=-1, keepdims=True)
        return (x * lax.rsqrt(ms + EPS) * g_ref[...]).astype(BF16)

    h = normed(x_ref[...])
    qkv = jnp.dot(h, wqkv_ref[...], preferred_element_type=F32)

    qk = qkv[:, :QK_COLS]
    msq = jnp.dot((qk * qk).astype(BF16), mavg_ref[...], preferred_element_type=F32)

    has_prev = (t0 > 0).astype(F32)
    has_next = (t0 + tm < seq_len).astype(F32)
    h_ext = jnp.concatenate([normed(xp_ref[...] * has_prev), h, normed(xn_ref[...] * has_next)],
                            axis=0)
    u_scr[...] = jnp.dot(h_ext, why_ref[...], preferred_element_type=F32)
    cw = cw_ref[...]
    uc = (cb_ref[...] + u_scr[HALO - 1:HALO - 1 + tm, :] * cw[0:1]
          + u_scr[HALO:HALO + tm, :] * cw[1:2] + u_scr[HALO + 1:HALO + 1 + tm, :] * cw[2:3])
    x0_ref[...] = uc[:, :HY_WIDTH]
    zt_ref[...] = (uc[:, HY_WIDTH:2 * HY_WIDTH] * uc[:, 2 * HY_WIDTH:]).T

    qkn = qk * lax.rsqrt(msq + EPS) * gqk_ref[...]

    t = t0 + lax.broadcasted_iota(jnp.int32, (tm, 1), 0)
    row = (t // GRID_W).astype(F32)
    col = (t % GRID_W).astype(F32)
    ang = jnp.where(isrow_ref[...] > 0.5, row, col) * inv_ref[...]
    cos = jnp.cos(ang)
    sin = jnp.sin(ang)
    lane = lax.broadcasted_iota(jnp.int32, (1, LANES), 1)
    first_half = (lane % HEAD_DIM) < (HEAD_DIM // 2)
    sin_signed = jnp.where(first_half, -sin, sin)

    def rope(xg):
        partner = jnp.where(first_half, pltpu.roll(xg, LANES - HEAD_DIM // 2, 1),
                            pltpu.roll(xg, HEAD_DIM // 2, 1))
        return xg * cos + partner * sin_signed

    for j in range(Q_COLS // LANES):
        r = rope(qkn[:, j * LANES:(j + 1) * LANES])
        q_ref[j * LANES:(j + 1) * LANES, :] = r.T.astype(BF16)
    r = rope(qkn[:, Q_COLS:QK_COLS]).astype(BF16)
    k_ref[0] = r[:, :HEAD_DIM]
    k_ref[1] = r[:, HEAD_DIM:]
    vt = qkv[:, QK_COLS:].T.astype(BF16)
    ones = jnp.ones((V_EXT - HEAD_DIM, tm), BF16)
    for g in range(N_KV_HEADS):
        v_ref[g * V_EXT:g * V_EXT + HEAD_DIM, :] = vt[g * HEAD_DIM:(g + 1) * HEAD_DIM, :]
        v_ref[g * V_EXT + HEAD_DIM:(g + 1) * V_EXT, :] = ones


def _inproj(x2, g, w_hy, w_qkv, gqk, mavg, inv_lane, isrow_lane, conv_w, conv_b, seq_len):
    n = x2.shape[0]
    tm = TOKEN_TILE
    halo_per_tile = tm // HALO
    last_halo = n // HALO - 1
    return pl.pallas_call(
        functools.partial(_inproj_kernel, seq_len=seq_len),
        grid=(n // tm,),
        in_specs=[
            pl.BlockSpec((tm, D_MODEL), lambda i: (i, 0)),
            pl.BlockSpec((HALO, D_MODEL), lambda i: (jnp.maximum(i * halo_per_tile - 1, 0), 0)),
            pl.BlockSpec((HALO, D_MODEL),
                         lambda i: (jnp.minimum((i + 1) * halo_per_tile, last_halo), 0)),
            _const_spec((1, D_MODEL)),
            _const_spec((D_MODEL, HY_COLS)),
            _const_spec((D_MODEL, QK_COLS + KV_COLS)),
            _const_spec((1, QK_COLS)),
            _const_spec((QK_COLS, QK_COLS)),
            _const_spec((1, LANES)),
            _const_spec((1, LANES)),
            _const_spec((SHORT_CONV, HY_COLS)),
            _const_spec((1, HY_COLS)),
        ],
        out_specs=[
            pl.BlockSpec((HY_WIDTH, tm), lambda i: (0, i)),
            pl.BlockSpec((tm, HY_WIDTH), lambda i: (i, 0)),
            pl.BlockSpec((Q_COLS, tm), lambda i: (0, i)),
            pl.BlockSpec((N_KV_HEADS, tm, HEAD_DIM), lambda i: (0, i, 0)),
            pl.BlockSpec((N_KV_HEADS * V_EXT, tm), lambda i: (0, i)),
        ],
        out_shape=[
            jax.ShapeDtypeStruct((HY_WIDTH, n), F32),
            jax.ShapeDtypeStruct((n, HY_WIDTH), F32),
            jax.ShapeDtypeStruct((Q_COLS, n), BF16),
            jax.ShapeDtypeStruct((N_KV_HEADS, n, HEAD_DIM), BF16),
            jax.ShapeDtypeStruct((N_KV_HEADS * V_EXT, n), BF16),
        ],
        scratch_shapes=[pltpu.VMEM((tm + 2 * HALO, HY_COLS), F32)],
        compiler_params=_params("parallel"),
        name="inproj",
    )(x2, x2, x2, g, w_hy, w_qkv, gqk, mavg, inv_lane, isrow_lane, conv_w, conv_b)


def _filter_kernel(w1t_ref, w1c_ref, w1s_ref, b1_ref, w2_ref, b2_ref, fr_ref, bands_ref,
                   w3_ref, decay_ref, k_ref, ksum_ref, *, seq_len):
    tl = k_ref.shape[1]
    hp = lax.Precision.HIGHEST
    p = pl.program_id(0) * tl + lax.broadcasted_iota(jnp.int32, (1, tl), 1)
    n = jnp.abs(p - seq_len).astype(F32)
    t = n * (1.0 / (seq_len - 1))
    w = n * (2.0 * math.pi / seq_len)
    ang = bands_ref[...] * w
    pre = (w1t_ref[...] * t
           + jnp.dot(w1c_ref[...], jnp.cos(ang), precision=hp, preferred_element_type=F32)
           - jnp.dot(w1s_ref[...], jnp.sin(ang), precision=hp, preferred_element_type=F32)
           + b1_ref[...])
    fr = fr_ref[...]
    h = jnp.sin(fr * pre)
    h = jnp.sin(fr * (jnp.dot(w2_ref[...], h, precision=hp, preferred_element_type=F32)
                      + b2_ref[...]))
    h = jnp.dot(w3_ref[...], h, precision=hp, preferred_element_type=F32)
    h = h * jnp.exp(-t * jnp.abs(decay_ref[...]))
    h = jnp.where(p == 0, 0.0, h)
    k_ref[...] = h

    @pl.when(pl.program_id(0) == 0)
    def _():
        ksum_ref[...] = jnp.zeros_like(ksum_ref)

    ksum_ref[...] += jnp.broadcast_to(jnp.sum(jnp.abs(h), axis=1, keepdims=True), ksum_ref.shape)


def _hyena_filter(seq_len, w1, b1, w2, b2, w3, freq, decay):
    tl = FILTER_TILE
    n_tiles = 2 * seq_len // tl
    half = seq_len // tl
    col = lambda a: a.reshape(-1, 1)
    bands = jnp.linspace(1e-4, FILTER_BANDS - 1, FILTER_BANDS, dtype=F32).reshape(-1, 1)
    w1t = w1.T
    args = (w1t[:, :1], w1t[:, 1:1 + FILTER_BANDS], w1t[:, 1 + FILTER_BANDS:], col(b1),
            w2.T, col(b2), col(freq), jnp.asarray(bands),
            w3.T, decay.reshape(-1, 1))
    dir_map = lambda i: (jnp.where(i >= half, 0, 1), 0)
    in_specs = [_const_spec(a.shape) for a in args[:8]]
    in_specs += [pl.BlockSpec((HY_WIDTH, FILTER_HIDDEN), dir_map),
                 pl.BlockSpec((HY_WIDTH, 1), dir_map)]
    return pl.pallas_call(
        functools.partial(_filter_kernel, seq_len=seq_len),
        grid=(n_tiles,),
        in_specs=in_specs,
        out_specs=[pl.BlockSpec((HY_WIDTH, tl), lambda i: (0, i)),
                   pl.BlockSpec((HY_WIDTH, LANES), lambda i: (0, 0))],
        out_shape=[jax.ShapeDtypeStruct((HY_WIDTH, 2 * seq_len), F32),
                   jax.ShapeDtypeStruct((HY_WIDTH, LANES), F32)],
        compiler_params=_params("arbitrary"),
        name="hyena_filter",
    )(*args)


def _longconv_kernel(z_ref, k_ref, ksum_ref, d_ref, y_ref, zrows2, acc2, tsub2):
    ncb, bsz, seq_len = z_ref.shape
    p = CONV_BLOCK
    q = CONV_SUB
    nb = seq_len // p

    def channel(c, slot):
        zrows, acc, tsub = zrows2.at[slot], acc2.at[slot], tsub2.at[slot]
        zc = z_ref[c]
        for i in range(nb):
            zrows[i * bsz:(i + 1) * bsz, :] = zc[:, i * p:(i + 1) * p]
        acc[...] = jnp.zeros_like(acc)
        inv = 1.0 / ksum_ref[c][:, :1]
        for lo in range(0, 2 * seq_len, SKEW_CHUNK):
            src_lo = max(lo - q, 0)
            kn = k_ref[c, :, src_lo:lo + SKEW_CHUNK] * inv
            skew = pltpu.roll(jnp.broadcast_to(kn, (q, kn.shape[1])), 0, 1,
                              stride=1, stride_axis=0)
            tsub[:, lo:lo + SKEW_CHUNK] = skew[:, lo - src_lo:].astype(BF16)

        def sub_tile(m):
            return tsub[:, seq_len + m * q:seq_len + (m + 1) * q]

        for d in range(-(nb - 1), nb):
            t_mid = sub_tile(2 * d)
            w_d = jnp.concatenate(
                [jnp.concatenate([t_mid, sub_tile(2 * d + 1)], axis=1),
                 jnp.concatenate([sub_tile(2 * d - 1), t_mid], axis=1)], axis=0)
            j0, j1 = max(0, -d), min(nb, nb - d)
            lhs = zrows[j0 * bsz:j1 * bsz, :].astype(BF16)
            acc[(j0 + d) * bsz:(j1 + d) * bsz, :] += jnp.dot(lhs, w_d, preferred_element_type=F32)
        dc = d_ref[c][:, :1]
        for i in range(nb):
            y_ref[c, :, i * p:(i + 1) * p] = (acc[i * bsz:(i + 1) * bsz, :]
                                              + zc[:, i * p:(i + 1) * p] * dc)

    def channel_pair(i, carry):
        channel(2 * i, 0)
        channel(2 * i + 1, 1)
        return carry

    lax.fori_loop(0, ncb // 2, channel_pair, 0)


def _longconv(zt3, kext, ksum, d_bias):
    c, bsz, seq_len = zt3.shape
    cb = CONV_CH
    nb = seq_len // CONV_BLOCK
    return pl.pallas_call(
        _longconv_kernel,
        grid=(c // cb,),
        in_specs=[pl.BlockSpec((cb, bsz, seq_len), lambda i: (i, 0, 0)),
                  pl.BlockSpec((cb, 1, 2 * seq_len), lambda i: (i, 0, 0)),
                  pl.BlockSpec((cb, 1, LANES), lambda i: (i, 0, 0)),
                  pl.BlockSpec((cb, 1, LANES), lambda i: (i, 0, 0))],
        out_specs=pl.BlockSpec((cb, bsz, seq_len), lambda i: (i, 0, 0)),
        out_shape=jax.ShapeDtypeStruct((c, bsz, seq_len), F32),
        scratch_shapes=[pltpu.VMEM((2, nb * bsz, CONV_BLOCK), F32),
                        pltpu.VMEM((2, nb * bsz, CONV_BLOCK), F32),
                        pltpu.VMEM((2, CONV_SUB, 2 * seq_len), BF16)],
        compiler_params=_params("parallel"),
        name="hyena_longconv",
    )(zt3, kext.reshape(c, 1, 2 * seq_len), ksum.reshape(c, 1, LANES), d_bias)


def _attn_kernel(q_ref, k_ref, v_ref, o_ref, m_ref, acc_ref, *, kv_tile):
    seq_len = k_ref.shape[1]
    m_ref[...] = jnp.full(m_ref.shape, -jnp.inf, F32)
    acc_ref[...] = jnp.zeros(acc_ref.shape, F32)

    def step(c, carry):
        off = pl.multiple_of(c * kv_tile, kv_tile)
        kc = k_ref[0, pl.ds(off, kv_tile), :]
        vc = v_ref[:, pl.ds(off, kv_tile)]
        for h in range(GROUP):
            qh = q_ref[h * HEAD_DIM:(h + 1) * HEAD_DIM, :]
            s = jnp.dot(kc, qh, preferred_element_type=F32)
            m_old = m_ref[h]
            m_new = jnp.maximum(m_old, jnp.max(s, axis=0, keepdims=True))
            alpha = jnp.exp(m_old - m_new)
            p = jnp.exp(s - m_new).astype(BF16)
            acc_ref[h] = alpha * acc_ref[h] + jnp.dot(vc, p, preferred_element_type=F32)
            m_ref[h] = m_new
        return carry

    lax.fori_loop(0, seq_len // kv_tile, step, 0)
    for h in range(GROUP):
        a = acc_ref[h]
        o_ref[h * HEAD_DIM:(h + 1) * HEAD_DIM, :] = a[:HEAD_DIM] / a[HEAD_DIM:HEAD_DIM + 1]


def _attn_bounded_kernel(q_ref, k_ref, v_ref, o_ref, qcat_ref, acc_ref, *, kv_tile):
    seq_len = k_ref.shape[1]
    tq = q_ref.shape[1]
    acc_ref[...] = jnp.zeros(acc_ref.shape, F32)
    for h in range(GROUP):
        qcat_ref[:, h * tq:(h + 1) * tq] = q_ref[h * HEAD_DIM:(h + 1) * HEAD_DIM, :]

    def step(c, carry):
        off = pl.multiple_of(c * kv_tile, kv_tile)
        kc = k_ref[0, pl.ds(off, kv_tile), :]
        vc = v_ref[:, pl.ds(off, kv_tile)]
        p = jnp.exp(jnp.dot(kc, qcat_ref[...], preferred_element_type=F32)).astype(BF16)
        acc_ref[...] += jnp.dot(vc, p, preferred_element_type=F32)
        return carry

    lax.fori_loop(0, seq_len // kv_tile, step, 0)
    for h in range(GROUP):
        a = acc_ref[:, h * tq:(h + 1) * tq]
        o_ref[h * HEAD_DIM:(h + 1) * HEAD_DIM, :] = a[:HEAD_DIM] / a[HEAD_DIM:HEAD_DIM + 1]


def _attention(qt, k, vt, bsz, seq_len, score_bound):
    n = bsz * seq_len
    tq = ATT_Q_TILE
    nq = seq_len // tq
    kv_tile = min(ATT_KV_TILE, seq_len)
    gw = GROUP * HEAD_DIM

    def call(body, scratch, name):
        return pl.pallas_call(
            functools.partial(body, kv_tile=kv_tile),
            grid=(bsz, N_KV_HEADS, nq),
            in_specs=[pl.BlockSpec((gw, tq), lambda b, h, i: (h, b * nq + i)),
                      pl.BlockSpec((1, seq_len, HEAD_DIM), lambda b, h, i: (h, b, 0)),
                      pl.BlockSpec((V_EXT, seq_len), lambda b, h, i: (h, b))],
            out_specs=pl.BlockSpec((gw, tq), lambda b, h, i: (h, b * nq + i)),
            out_shape=jax.ShapeDtypeStruct((ATT_WIDTH, n), F32),
            scratch_shapes=scratch,
            compiler_params=_params("parallel", "parallel", "parallel"),
            name=name,
        )

    acc = pltpu.VMEM((GROUP, V_EXT, tq), F32)
    bounded = call(_attn_bounded_kernel,
                   [pltpu.VMEM((HEAD_DIM, GROUP * tq), BF16), pltpu.VMEM((V_EXT, GROUP * tq), F32)],
                   "attention_bounded")
    online = call(_attn_kernel, [pltpu.VMEM((GROUP, 1, tq), F32), acc], "attention_online")
    return lax.cond(score_bound <= SCORE_BOUND_MAX, bounded, online, qt, k, vt)


def _rms(x, g):
    return x * lax.rsqrt(jnp.mean(x * x, axis=-1, keepdims=True) + EPS) * g


def _post_kernel(x_ref, yt_ref, x0_ref, ya_ref, ghy_ref, gatt_ref, wout_ref, gffn_ref,
                 wg_ref, wu_ref, wd_ref, gfin_ref, o_ref):
    y_hy = x0_ref[...] * yt_ref[...].T
    mixed_in = jnp.concatenate([_rms(y_hy, ghy_ref[...]), _rms(ya_ref[...].T, gatt_ref[...])],
                               axis=-1).astype(BF16)
    x1 = x_ref[...] + jnp.dot(mixed_in, wout_ref[...], preferred_element_type=F32)
    h = _rms(x1, gffn_ref[...]).astype(BF16)
    gate = jnp.dot(h, wg_ref[...], preferred_element_type=F32)
    up = jnp.dot(h, wu_ref[...], preferred_element_type=F32)
    act = (gate * jax.nn.sigmoid(gate) * up).astype(BF16)
    x2 = x1 + jnp.dot(act, wd_ref[...], preferred_element_type=F32)
    o_ref[...] = _rms(x2, gfin_ref[...])


def _post(x2, yt, x0c, y_att, g_hy, g_att, w_out, g_ffn, w_gate, w_up, w_down, g_fin):
    n = x2.shape[0]
    tm = POST_TILE
    return pl.pallas_call(
        _post_kernel,
        grid=(n // tm,),
        in_specs=[
            pl.BlockSpec((tm, D_MODEL), lambda i: (i, 0)),
            pl.BlockSpec((HY_WIDTH, tm), lambda i: (0, i)),
            pl.BlockSpec((tm, HY_WIDTH), lambda i: (i, 0)),
            pl.BlockSpec((ATT_WIDTH, tm), lambda i: (0, i)),
            _const_spec((1, HY_WIDTH)),
            _const_spec((1, ATT_WIDTH)),
            _const_spec((D_MODEL, D_MODEL)),
            _const_spec((1, D_MODEL)),
            _const_spec((D_MODEL, D_FF)),
            _const_spec((D_MODEL, D_FF)),
            _const_spec((D_FF, D_MODEL)),
            _const_spec((1, D_MODEL)),
        ],
        out_specs=pl.BlockSpec((tm, D_MODEL), lambda i: (i, 0)),
        out_shape=jax.ShapeDtypeStruct((n, D_MODEL), F32),
        compiler_params=_params("parallel"),
        name="post",
    )(x2, yt, x0c, y_att, g_hy, g_att, w_out, g_ffn, w_gate, w_up, w_down, g_fin)


def _rope_lane_constants():
    lane = np.arange(LANES)
    pair = lane % (HEAD_DIM // 2)
    inv = ROPE_THETA ** (-jnp.arange(0, AXIS_DIM, 2, dtype=F32) / AXIS_DIM)
    isrow = (pair < AXIS_DIM // 2).astype(np.float32)
    return inv[pair % (AXIS_DIM // 2)].reshape(1, LANES), jnp.asarray(isrow.reshape(1, LANES))


def _prepare_weights(norm_mix_g, w_in, hy_conv_w, hy_conv_b, hy_d, q_norm_g, k_norm_g,
                     hy_out_g, att_out_g, w_out, norm_ffn_g, w_gate, w_up, w_down, final_norm_g):
    row = lambda a: a.reshape(1, -1).astype(F32)
    perm = np.concatenate([np.arange(0, HEAD_DIM, 2), np.arange(1, HEAD_DIM, 2)])
    qk_perm = np.concatenate([h * HEAD_DIM + perm for h in range(N_HEADS + N_KV_HEADS)])
    w_hy = w_in[:, :HY_COLS].astype(BF16)
    w_qk = w_in[:, HY_COLS:HY_COLS + QK_COLS][:, qk_perm]
    w_qkv = jnp.concatenate([w_qk, w_in[:, HY_COLS + QK_COLS:]], axis=1).astype(BF16)
    scale = HEAD_DIM ** -0.5
    gqk = jnp.concatenate([jnp.tile(q_norm_g[perm] * scale, N_HEADS),
                           jnp.tile(k_norm_g[perm], N_KV_HEADS)]).reshape(1, QK_COLS)
    head_id = np.arange(QK_COLS) // HEAD_DIM
    mavg = jnp.asarray((head_id[:, None] == head_id[None, :]).astype(np.float32) / HEAD_DIM,
                       dtype=BF16)
    inv_lane, isrow_lane = _rope_lane_constants()
    d_bias = jnp.broadcast_to(hy_d.reshape(HY_WIDTH, 1, 1), (HY_WIDTH, 1, LANES))
    score_bound = (HEAD_DIM ** 0.5) * jnp.max(jnp.abs(q_norm_g)) * jnp.max(jnp.abs(k_norm_g))
    return dict(
        score_bound=score_bound,
        inproj=(row(norm_mix_g), w_hy, w_qkv, gqk, mavg, inv_lane, isrow_lane, hy_conv_w,
                row(hy_conv_b)),
        d_bias=d_bias,
        post=(row(hy_out_g), row(att_out_g), w_out.astype(BF16), row(norm_ffn_g),
              w_gate.astype(BF16), w_up.astype(BF16), w_down.astype(BF16), row(final_norm_g)),
    )


def _trunk(x, wts, filt_params):
    bsz, seq_len, _ = x.shape
    n = bsz * seq_len
    x2 = x.reshape(n, D_MODEL)
    zt, x0c, q, k, v = _inproj(x2, *wts["inproj"], seq_len)
    kext, ksum = _hyena_filter(seq_len, *filt_params)
    yt = _longconv(zt.reshape(HY_WIDTH, bsz, seq_len), kext, ksum, wts["d_bias"])
    y_att = _attention(q, k, v, bsz, seq_len, wts["score_bound"])
    out = _post(x2, yt.reshape(HY_WIDTH, n), x0c, y_att, *wts["post"])
    return out.reshape(bsz, seq_len, D_MODEL)


def kernel(x_prompt, x_sample, norm_mix_g, w_in, hy_conv_w, hy_conv_b, hy_f_w1, hy_f_b1, hy_f_w2,
           hy_f_b2, hy_f_w3, hy_f_freq, hy_decay, hy_d, q_norm_g, k_norm_g, hy_out_g, att_out_g,
           w_out, norm_ffn_g, w_gate, w_up, w_down, final_norm_g):
    wts = _prepare_weights(norm_mix_g[0], w_in[0], hy_conv_w[0], hy_conv_b[0], hy_d[0],
                           q_norm_g[0], k_norm_g[0], hy_out_g[0], att_out_g[0], w_out[0],
                           norm_ffn_g[0], w_gate[0], w_up[0], w_down[0], final_norm_g)
    filt_params = (hy_f_w1[0], hy_f_b1[0], hy_f_w2[0], hy_f_b2[0], hy_f_w3[0], hy_f_freq[0],
                   hy_decay[0])
    return (_trunk(x_prompt, wts, filt_params), _trunk(x_sample, wts, filt_params))
```

```python
import functools
import math

import jax
import jax.numpy as jnp
import numpy as np
from jax import lax
from jax.experimental import pallas as pl
from jax.experimental.pallas import tpu as pltpu

F32 = jnp.float32
BF16 = jnp.bfloat16

D_MODEL = 1024
HY_WIDTH = 512
ATT_WIDTH = 512
HEAD_DIM = 64
N_HEADS = 8
N_KV_HEADS = 2
GROUP = N_HEADS // N_KV_HEADS
HY_COLS = 3 * HY_WIDTH
Q_COLS = N_HEADS * HEAD_DIM
KV_COLS = N_KV_HEADS * HEAD_DIM
QK_COLS = Q_COLS + KV_COLS
SHORT_CONV = 3
FILTER_EMB = 33
FILTER_BANDS = 16
FILTER_HIDDEN = 64
GRID_W = 64
ROPE_THETA = 10000.0
AXIS_DIM = HEAD_DIM // 2
D_FF = 2816
EPS = 1e-6

LANES = 128
SUBLANES = 8
MXU_DIM = 256
VMEM_LIMIT_BYTES = 56 * 1024 * 1024

TOKEN_TILE = 1024
POST_TILE = 512
HALO = 16
FILTER_TILE = 512
CONV_BLOCK = MXU_DIM
CONV_SUB = CONV_BLOCK // 2
SKEW_CHUNK = 1024
CONV_CH = 8
ATT_Q_TILE = 512
ATT_KV_TILE = 2048
V_EXT = HEAD_DIM + 16
SCORE_BOUND_MAX = 40.0


def _params(*sem):
    return pltpu.CompilerParams(dimension_semantics=sem, vmem_limit_bytes=VMEM_LIMIT_BYTES)


def _const_spec(shape):
    nd = len(shape)
    return pl.BlockSpec(shape, lambda *_: (0,) * nd, pipeline_mode=pl.Buffered(1))


def _rope_kernel(inv_ref, isrow_ref, cos_ref, sin_ref):
    tl = cos_ref.shape[0]
    t = pl.program_id(0) * tl + lax.broadcasted_iota(jnp.int32, (tl, 1), 0)
    row = (t // GRID_W).astype(F32)
    col = (t % GRID_W).astype(F32)
    ang = jnp.where(isrow_ref[...] > 0.5, row, col) * inv_ref[...]
    lane = lax.broadcasted_iota(jnp.int32, (1, LANES), 1)
    first_half = (lane % HEAD_DIM) < (HEAD_DIM // 2)
    sin = jnp.sin(ang)
    cos_ref[...] = jnp.cos(ang)
    sin_ref[...] = jnp.where(first_half, -sin, sin)


def _rope_tables(seq_len, inv_lane, isrow_lane):
    tl = TOKEN_TILE
    return pl.pallas_call(
        _rope_kernel,
        grid=(seq_len // tl,),
        in_specs=[_const_spec((1, LANES)), _const_spec((1, LANES))],
        out_specs=[pl.BlockSpec((tl, LANES), lambda i: (i, 0))] * 2,
        out_shape=[jax.ShapeDtypeStruct((seq_len, LANES), F32)] * 2,
        compiler_params=_params("parallel"),
        name="rope_tables",
    )(inv_lane, isrow_lane)


def _inproj_kernel(x_ref, xp_ref, xn_ref, g_ref, why_ref, wqkv_ref, gqk_ref, mavg_ref, cos_ref,
                   sin_ref, cw_ref, cb_ref, zt_ref, x0_ref, q_ref, k_ref, v_ref, u_scr, *,
                   seq_len):
    tm = x_ref.shape[0]
    t0 = (pl.program_id(0) * tm) % seq_len

    def normed(x):
        ms = jnp.mean(x * x, axis=-1, keepdims=True)
        return (x * lax.rsqrt(ms + EPS) * g_ref[...]).astype(BF16)

    h = normed(x_ref[...])
    qkv = jnp.dot(h, wqkv_ref[...], preferred_element_type=F32)

    qk = qkv[:, :QK_COLS]
    msq = jnp.dot((qk * qk).astype(BF16), mavg_ref[...], preferred_element_type=F32)

    has_prev = (t0 > 0).astype(F32)
    has_next = (t0 + tm < seq_len).astype(F32)
    h_ext = jnp.concatenate([normed(xp_ref[...] * has_prev), h, normed(xn_ref[...] * has_next)],
                            axis=0)
    u_scr[...] = jnp.dot(h_ext, why_ref[...], preferred_element_type=F32)
    cw = cw_ref[...]
    uc = (cb_ref[...] + u_scr[HALO - 1:HALO - 1 + tm, :] * cw[0:1]
          + u_scr[HALO:HALO + tm, :] * cw[1:2] + u_scr[HALO + 1:HALO + 1 + tm, :] * cw[2:3])
    x0_ref[...] = uc[:, :HY_WIDTH]
    zt_ref[...] = (uc[:, HY_WIDTH:2 * HY_WIDTH] * uc[:, 2 * HY_WIDTH:]).T

    qkn = qk * lax.rsqrt(msq + EPS) * gqk_ref[...]

    cos = cos_ref[...]
    sin_signed = sin_ref[...]
    lane = lax.broadcasted_iota(jnp.int32, (1, LANES), 1)
    first_half = (lane % HEAD_DIM) < (HEAD_DIM // 2)

    def rope(xg):
        partner = jnp.where(first_half, pltpu.roll(xg, LANES - HEAD_DIM // 2, 1),
                            pltpu.roll(xg, HEAD_DIM // 2, 1))
        return xg * cos + partner * sin_signed

    for j in range(Q_COLS // LANES):
        r = rope(qkn[:, j * LANES:(j + 1) * LANES])
        q_ref[j * LANES:(j + 1) * LANES, :] = r.T.astype(BF16)
    r = rope(qkn[:, Q_COLS:QK_COLS]).astype(BF16)
    k_ref[0] = r[:, :HEAD_DIM]
    k_ref[1] = r[:, HEAD_DIM:]
    vt = qkv[:, QK_COLS:].T.astype(BF16)
    ones = jnp.ones((V_EXT - HEAD_DIM, tm), BF16)
    for g in range(N_KV_HEADS):
        v_ref[g * V_EXT:g * V_EXT + HEAD_DIM, :] = vt[g * HEAD_DIM:(g + 1) * HEAD_DIM, :]
        v_ref[g * V_EXT + HEAD_DIM:(g + 1) * V_EXT, :] = ones


def _inproj(x2, g, w_hy, w_qkv, gqk, mavg, inv_lane, isrow_lane, conv_w, conv_b, seq_len):
    n = x2.shape[0]
    tm = TOKEN_TILE
    halo_per_tile = tm // HALO
    last_halo = n // HALO - 1
    tiles_per_seq = seq_len // tm
    cos, sin_signed = _rope_tables(seq_len, inv_lane, isrow_lane)
    return pl.pallas_call(
        functools.partial(_inproj_kernel, seq_len=seq_len),
        grid=(n // tm,),
        in_specs=[
            pl.BlockSpec((tm, D_MODEL), lambda i: (i, 0)),
            pl.BlockSpec((HALO, D_MODEL), lambda i: (jnp.maximum(i * halo_per_tile - 1, 0), 0)),
            pl.BlockSpec((HALO, D_MODEL),
                         lambda i: (jnp.minimum((i + 1) * halo_per_tile, last_halo), 0)),
            _const_spec((1, D_MODEL)),
            _const_spec((D_MODEL, HY_COLS)),
            _const_spec((D_MODEL, QK_COLS + KV_COLS)),
            _const_spec((1, QK_COLS)),
            _const_spec((QK_COLS, QK_COLS)),
            pl.BlockSpec((tm, LANES), lambda i: (i % tiles_per_seq, 0)),
            pl.BlockSpec((tm, LANES), lambda i: (i % tiles_per_seq, 0)),
            _const_spec((SHORT_CONV, HY_COLS)),
            _const_spec((1, HY_COLS)),
        ],
        out_specs=[
            pl.BlockSpec((HY_WIDTH, tm), lambda i: (0, i)),
            pl.BlockSpec((tm, HY_WIDTH), lambda i: (i, 0)),
            pl.BlockSpec((Q_COLS, tm), lambda i: (0, i)),
            pl.BlockSpec((N_KV_HEADS, tm, HEAD_DIM), lambda i: (0, i, 0)),
            pl.BlockSpec((N_KV_HEADS * V_EXT, tm), lambda i: (0, i)),
        ],
        out_shape=[
            jax.ShapeDtypeStruct((HY_WIDTH, n), F32),
            jax.ShapeDtypeStruct((n, HY_WIDTH), F32),
            jax.ShapeDtypeStruct((Q_COLS, n), BF16),
            jax.ShapeDtypeStruct((N_KV_HEADS, n, HEAD_DIM), BF16),
            jax.ShapeDtypeStruct((N_KV_HEADS * V_EXT, n), BF16),
        ],
        scratch_shapes=[pltpu.VMEM((tm + 2 * HALO, HY_COLS), F32)],
        compiler_params=_params("parallel"),
        name="inproj",
    )(x2, x2, x2, g, w_hy, w_qkv, gqk, mavg, cos, sin_signed, conv_w, conv_b)


def _filter_kernel(w1t_ref, w1c_ref, w1s_ref, b1_ref, w2_ref, b2_ref, fr_ref, bands_ref,
                   w3_ref, decay_ref, k_ref, ksum_ref, *, seq_len):
    tl = k_ref.shape[1]
    hp = lax.Precision.HIGHEST
    p = pl.program_id(0) * tl + lax.broadcasted_iota(jnp.int32, (1, tl), 1)
    n = jnp.abs(p - seq_len).astype(F32)
    t = n * (1.0 / (seq_len - 1))
    w = n * (2.0 * math.pi / seq_len)
    ang = bands_ref[...] * w
    pre = (w1t_ref[...] * t
           + jnp.dot(w1c_ref[...], jnp.cos(ang), precision=hp, preferred_element_type=F32)
           - jnp.dot(w1s_ref[...], jnp.sin(ang), precision=hp, preferred_element_type=F32)
           + b1_ref[...])
    fr = fr_ref[...]
    h = jnp.sin(fr * pre)
    h = jnp.sin(fr * (jnp.dot(w2_ref[...], h, precision=hp, preferred_element_type=F32)
                      + b2_ref[...]))
    h = jnp.dot(w3_ref[...], h, precision=hp, preferred_element_type=F32)
    h = h * jnp.exp(-t * jnp.abs(decay_ref[...]))
    h = jnp.where(p == 0, 0.0, h)
    k_ref[...] = h

    @pl.when(pl.program_id(0) == 0)
    def _():
        ksum_ref[...] = jnp.zeros_like(ksum_ref)

    ksum_ref[...] += jnp.broadcast_to(jnp.sum(jnp.abs(h), axis=1, keepdims=True), ksum_ref.shape)


def _hyena_filter(seq_len, w1, b1, w2, b2, w3, freq, decay):
    tl = FILTER_TILE
    n_tiles = 2 * seq_len // tl
    half = seq_len // tl
    col = lambda a: a.reshape(-1, 1)
    bands = jnp.linspace(1e-4, FILTER_BANDS - 1, FILTER_BANDS, dtype=F32).reshape(-1, 1)
    w1t = w1.T
    args = (w1t[:, :1], w1t[:, 1:1 + FILTER_BANDS], w1t[:, 1 + FILTER_BANDS:], col(b1),
            w2.T, col(b2), col(freq), jnp.asarray(bands),
            w3.T, decay.reshape(-1, 1))
    dir_map = lambda i: (jnp.where(i >= half, 0, 1), 0)
    in_specs = [_const_spec(a.shape) for a in args[:8]]
    in_specs += [pl.BlockSpec((HY_WIDTH, FILTER_HIDDEN), dir_map),
                 pl.BlockSpec((HY_WIDTH, 1), dir_map)]
    return pl.pallas_call(
        functools.partial(_filter_kernel, seq_len=seq_len),
        grid=(n_tiles,),
        in_specs=in_specs,
        out_specs=[pl.BlockSpec((HY_WIDTH, tl), lambda i: (0, i)),
                   pl.BlockSpec((HY_WIDTH, LANES), lambda i: (0, 0))],
        out_shape=[jax.ShapeDtypeStruct((HY_WIDTH, 2 * seq_len), F32),
                   jax.ShapeDtypeStruct((HY_WIDTH, LANES), F32)],
        compiler_params=_params("arbitrary"),
        name="hyena_filter",
    )(*args)


def _longconv_kernel(z_ref, k_ref, ksum_ref, d_ref, y_ref, zrows2, acc2, tsub2):
    ncb, bsz, seq_len = z_ref.shape
    p = CONV_BLOCK
    q = CONV_SUB
    nb = seq_len // p

    def channel(c, slot):
        zrows, acc, tsub = zrows2.at[slot], acc2.at[slot], tsub2.at[slot]
        zc = z_ref[c]
        for i in range(nb):
            zrows[i * bsz:(i + 1) * bsz, :] = zc[:, i * p:(i + 1) * p]
        acc[...] = jnp.zeros_like(acc)
        inv = 1.0 / ksum_ref[pl.ds(c, 1), :][:, :1]
        for lo in range(0, 2 * seq_len, SKEW_CHUNK):
            src_lo = max(lo - q, 0)
            kn = k_ref[pl.ds(c, 1), src_lo:lo + SKEW_CHUNK] * inv
            skew = pltpu.roll(jnp.broadcast_to(kn, (q, kn.shape[1])), 0, 1,
                              stride=1, stride_axis=0)
            tsub[:, lo:lo + SKEW_CHUNK] = skew[:, lo - src_lo:].astype(BF16)

        def sub_tile(m):
            return tsub[:, seq_len + m * q:seq_len + (m + 1) * q]

        for d in range(-(nb - 1), nb):
            t_mid = sub_tile(2 * d)
            w_d = jnp.concatenate(
                [jnp.concatenate([t_mid, sub_tile(2 * d + 1)], axis=1),
                 jnp.concatenate([sub_tile(2 * d - 1), t_mid], axis=1)], axis=0)
            j0, j1 = max(0, -d), min(nb, nb - d)
            lhs = zrows[j0 * bsz:j1 * bsz, :].astype(BF16)
            acc[(j0 + d) * bsz:(j1 + d) * bsz, :] += jnp.dot(lhs, w_d, preferred_element_type=F32)
        dc = d_ref[pl.ds(c, 1), :][:, :1]
        for i in range(nb):
            y_ref[c, :, i * p:(i + 1) * p] = (acc[i * bsz:(i + 1) * bsz, :]
                                              + zc[:, i * p:(i + 1) * p] * dc)

    def channel_pair(i, carry):
        channel(2 * i, 0)
        channel(2 * i + 1, 1)
        return carry

    lax.fori_loop(0, ncb // 2, channel_pair, 0)


def _longconv(zt3, kext, ksum, d_bias):
    c, bsz, seq_len = zt3.shape
    cb = CONV_CH
    nb = seq_len // CONV_BLOCK
    return pl.pallas_call(
        _longconv_kernel,
        grid=(c // cb,),
        in_specs=[pl.BlockSpec((cb, bsz, seq_len), lambda i: (i, 0, 0)),
                  pl.BlockSpec((cb, 2 * seq_len), lambda i: (i, 0)),
                  pl.BlockSpec((cb, LANES), lambda i: (i, 0)),
                  pl.BlockSpec((cb, LANES), lambda i: (i, 0))],
        out_specs=pl.BlockSpec((cb, bsz, seq_len), lambda i: (i, 0, 0)),
        out_shape=jax.ShapeDtypeStruct((c, bsz, seq_len), F32),
        scratch_shapes=[pltpu.VMEM((2, nb * bsz, CONV_BLOCK), F32),
                        pltpu.VMEM((2, nb * bsz, CONV_BLOCK), F32),
                        pltpu.VMEM((2, CONV_SUB, 2 * seq_len), BF16)],
        compiler_params=_params("parallel"),
        name="hyena_longconv",
    )(zt3, kext, ksum, d_bias)


def _attn_kernel(q_ref, k_ref, v_ref, o_ref, m_ref, acc_ref, *, kv_tile):
    seq_len = k_ref.shape[1]
    m_ref[...] = jnp.full(m_ref.shape, -jnp.inf, F32)
    acc_ref[...] = jnp.zeros(acc_ref.shape, F32)

    def step(c, carry):
        off = pl.multiple_of(c * kv_tile, kv_tile)
        kc = k_ref[0, pl.ds(off, kv_tile), :]
        vc = v_ref[:, pl.ds(off, kv_tile)]
        for h in range(GROUP):
            qh = q_ref[h * HEAD_DIM:(h + 1) * HEAD_DIM, :]
            s = jnp.dot(kc, qh, preferred_element_type=F32)
            m_old = m_ref[h]
            m_new = jnp.maximum(m_old, jnp.max(s, axis=0, keepdims=True))
            alpha = jnp.exp(m_old - m_new)
            p = jnp.exp(s - m_new).astype(BF16)
            acc_ref[h] = alpha * acc_ref[h] + jnp.dot(vc, p, preferred_element_type=F32)
            m_ref[h] = m_new
        return carry

    lax.fori_loop(0, seq_len // kv_tile, step, 0)
    for h in range(GROUP):
        a = acc_ref[h]
        o_ref[h * HEAD_DIM:(h + 1) * HEAD_DIM, :] = a[:HEAD_DIM] / a[HEAD_DIM:HEAD_DIM + 1]


def _attn_bounded_kernel(q_ref, k_ref, v_ref, o_ref, qcat_ref, acc_ref, *, kv_tile):
    seq_len = k_ref.shape[1]
    tq = q_ref.shape[1]
    acc_ref[...] = jnp.zeros(acc_ref.shape, F32)
    for h in range(GROUP):
        qcat_ref[:, h * tq:(h + 1) * tq] = q_ref[h * HEAD_DIM:(h + 1) * HEAD_DIM, :]

    def step(c, carry):
        off = pl.multiple_of(c * kv_tile, kv_tile)
        kc = k_ref[0, pl.ds(off, kv_tile), :]
        vc = v_ref[:, pl.ds(off, kv_tile)]
        p = jnp.exp(jnp.dot(kc, qcat_ref[...], preferred_element_type=F32)).astype(BF16)
        acc_ref[...] += jnp.dot(vc, p, preferred_element_type=F32)
        return carry

    lax.fori_loop(0, seq_len // kv_tile, step, 0)
    for h in range(GROUP):
        a = acc_ref[:, h * tq:(h + 1) * tq]
        o_ref[h * HEAD_DIM:(h + 1) * HEAD_DIM, :] = a[:HEAD_DIM] / a[HEAD_DIM:HEAD_DIM + 1]


def _attention(qt, k, vt, bsz, seq_len, score_bound):
    n = bsz * seq_len
    tq = ATT_Q_TILE
    nq = seq_len // tq
    kv_tile = min(ATT_KV_TILE, seq_len)
    gw = GROUP * HEAD_DIM

    def call(body, scratch, name):
        return pl.pallas_call(
            functools.partial(body, kv_tile=kv_tile),
            grid=(bsz, N_KV_HEADS, nq),
            in_specs=[pl.BlockSpec((gw, tq), lambda b, h, i: (h, b * nq + i)),
                      pl.BlockSpec((1, seq_len, HEAD_DIM), lambda b, h, i: (h, b, 0)),
                      pl.BlockSpec((V_EXT, seq_len), lambda b, h, i: (h, b))],
            out_specs=pl.BlockSpec((gw, tq), lambda b, h, i: (h, b * nq + i)),
            out_shape=jax.ShapeDtypeStruct((ATT_WIDTH, n), F32),
            scratch_shapes=scratch,
            compiler_params=_params("parallel", "parallel", "parallel"),
            name=name,
        )

    acc = pltpu.VMEM((GROUP, V_EXT, tq), F32)
    bounded = call(_attn_bounded_kernel,
                   [pltpu.VMEM((HEAD_DIM, GROUP * tq), BF16), pltpu.VMEM((V_EXT, GROUP * tq), F32)],
                   "attention_bounded")
    online = call(_attn_kernel, [pltpu.VMEM((GROUP, 1, tq), F32), acc], "attention_online")
    return lax.cond(score_bound <= SCORE_BOUND_MAX, bounded, online, qt, k, vt)


def _rms(x, g):
    return x * lax.rsqrt(jnp.mean(x * x, axis=-1, keepdims=True) + EPS) * g


def _post_kernel(x_ref, yt_ref, x0_ref, ya_ref, ghy_ref, gatt_ref, wout_ref, gffn_ref,
                 wg_ref, wu_ref, wd_ref, gfin_ref, o_ref):
    y_hy = x0_ref[...] * yt_ref[...].T
    mixed_in = jnp.concatenate([_rms(y_hy, ghy_ref[...]), _rms(ya_ref[...].T, gatt_ref[...])],
                               axis=-1).astype(BF16)
    x1 = x_ref[...] + jnp.dot(mixed_in, wout_ref[...], preferred_element_type=F32)
    h = _rms(x1, gffn_ref[...]).astype(BF16)
    gate = jnp.dot(h, wg_ref[...], preferred_element_type=F32)
    up = jnp.dot(h, wu_ref[...], preferred_element_type=F32)
    act = (gate * jax.nn.sigmoid(gate) * up).astype(BF16)
    x2 = x1 + jnp.dot(act, wd_ref[...], preferred_element_type=F32)
    o_ref[...] = _rms(x2, gfin_ref[...])


def _post(x2, yt, x0c, y_att, g_hy, g_att, w_out, g_ffn, w_gate, w_up, w_down, g_fin):
    n = x2.shape[0]
    tm = POST_TILE
    return pl.pallas_call(
        _post_kernel,
        grid=(n // tm,),
        in_specs=[
            pl.BlockSpec((tm, D_MODEL), lambda i: (i, 0)),
            pl.BlockSpec((HY_WIDTH, tm), lambda i: (0, i)),
            pl.BlockSpec((tm, HY_WIDTH), lambda i: (i, 0)),
            pl.BlockSpec((ATT_WIDTH, tm), lambda i: (0, i)),
            _const_spec((1, HY_WIDTH)),
            _const_spec((1, ATT_WIDTH)),
            _const_spec((D_MODEL, D_MODEL)),
            _const_spec((1, D_MODEL)),
            _const_spec((D_MODEL, D_FF)),
            _const_spec((D_MODEL, D_FF)),
            _const_spec((D_FF, D_MODEL)),
            _const_spec((1, D_MODEL)),
        ],
        out_specs=pl.BlockSpec((tm, D_MODEL), lambda i: (i, 0)),
        out_shape=jax.ShapeDtypeStruct((n, D_MODEL), F32),
        compiler_params=_params("parallel"),
        name="post",
    )(x2, yt, x0c, y_att, g_hy, g_att, w_out, g_ffn, w_gate, w_up, w_down, g_fin)


def _rope_lane_constants():
    lane = np.arange(LANES)
    pair = lane % (HEAD_DIM // 2)
    inv = ROPE_THETA ** (-jnp.arange(0, AXIS_DIM, 2, dtype=F32) / AXIS_DIM)
    isrow = (pair < AXIS_DIM // 2).astype(np.float32)
    return inv[pair % (AXIS_DIM // 2)].reshape(1, LANES), jnp.asarray(isrow.reshape(1, LANES))


def _prepare_weights(norm_mix_g, w_in, hy_conv_w, hy_conv_b, hy_d, q_norm_g, k_norm_g,
                     hy_out_g, att_out_g, w_out, norm_ffn_g, w_gate, w_up, w_down, final_norm_g):
    row = lambda a: a.reshape(1, -1).astype(F32)
    perm = np.concatenate([np.arange(0, HEAD_DIM, 2), np.arange(1, HEAD_DIM, 2)])
    qk_perm = np.concatenate([h * HEAD_DIM + perm for h in range(N_HEADS + N_KV_HEADS)])
    w_hy = w_in[:, :HY_COLS].astype(BF16)
    w_qk = w_in[:, HY_COLS:HY_COLS + QK_COLS][:, qk_perm]
    w_qkv = jnp.concatenate([w_qk, w_in[:, HY_COLS + QK_COLS:]], axis=1).astype(BF16)
    scale = HEAD_DIM ** -0.5
    gqk = jnp.concatenate([jnp.tile(q_norm_g[perm] * scale, N_HEADS),
                           jnp.tile(k_norm_g[perm], N_KV_HEADS)]).reshape(1, QK_COLS)
    head_id = np.arange(QK_COLS) // HEAD_DIM
    mavg = jnp.asarray((head_id[:, None] == head_id[None, :]).astype(np.float32) / HEAD_DIM,
                       dtype=BF16)
    inv_lane, isrow_lane = _rope_lane_constants()
    d_bias = jnp.broadcast_to(hy_d.reshape(HY_WIDTH, 1), (HY_WIDTH, LANES))
    score_bound = (HEAD_DIM ** 0.5) * jnp.max(jnp.abs(q_norm_g)) * jnp.max(jnp.abs(k_norm_g))
    return dict(
        score_bound=score_bound,
        inproj=(row(norm_mix_g), w_hy, w_qkv, gqk, mavg, inv_lane, isrow_lane, hy_conv_w,
                row(hy_conv_b)),
        d_bias=d_bias,
        post=(row(hy_out_g), row(att_out_g), w_out.astype(BF16), row(norm_ffn_g),
              w_gate.astype(BF16), w_up.astype(BF16), w_down.astype(BF16), row(final_norm_g)),
    )


def _trunk(x, wts, filt_params):
    bsz, seq_len, _ = x.shape
    n = bsz * seq_len
    x2 = x.reshape(n, D_MODEL)
    zt, x0c, q, k, v = _inproj(x2, *wts["inproj"], seq_len)
    kext, ksum = _hyena_filter(seq_len, *filt_params)
    yt = _longconv(zt.reshape(HY_WIDTH, bsz, seq_len), kext, ksum, wts["d_bias"])
    y_att = _attention(q, k, v, bsz, seq_len, wts["score_bound"])
    out = _post(x2, yt.reshape(HY_WIDTH, n), x0c, y_att, *wts["post"])
    return out.reshape(bsz, seq_len, D_MODEL)


def kernel(x_prompt, x_sample, norm_mix_g, w_in, hy_conv_w, hy_conv_b, hy_f_w1, hy_f_b1, hy_f_w2,
           hy_f_b2, hy_f_w3, hy_f_freq, hy_decay, hy_d, q_norm_g, k_norm_g, hy_out_g, att_out_g,
           w_out, norm_ffn_g, w_gate, w_up, w_down, final_norm_g):
    wts = _prepare_weights(norm_mix_g[0], w_in[0], hy_conv_w[0], hy_conv_b[0], hy_d[0],
                           q_norm_g[0], k_norm_g[0], hy_out_g[0], att_out_g[0], w_out[0],
                           norm_ffn_g[0], w_gate[0], w_up[0], w_down[0], final_norm_g)
    filt_params = (hy_f_w1[0], hy_f_b1[0], hy_f_w2[0], hy_f_b2[0], hy_f_w3[0], hy_f_freq[0],
                   hy_decay[0])
    return (_trunk(x_prompt, wts, filt_params), _trunk(x_sample, wts, filt_params))
```

```python
import functools
import math

import jax
import jax.numpy as jnp
import numpy as np
from jax import lax
from jax.experimental import pallas as pl
from jax.experimental.pallas import tpu as pltpu

F32 = jnp.float32
BF16 = jnp.bfloat16

D_MODEL = 1024
HY_WIDTH = 512
ATT_WIDTH = 512
HEAD_DIM = 64
N_HEADS = 8
N_KV_HEADS = 2
GROUP = N_HEADS // N_KV_HEADS
HY_COLS = 3 * HY_WIDTH
Q_COLS = N_HEADS * HEAD_DIM
KV_COLS = N_KV_HEADS * HEAD_DIM
QK_COLS = Q_COLS + KV_COLS
SHORT_CONV = 3
FILTER_EMB = 33
FILTER_BANDS = 16
FILTER_HIDDEN = 64
GRID_W = 64
ROPE_THETA = 10000.0
AXIS_DIM = HEAD_DIM // 2
D_FF = 2816
EPS = 1e-6

LANES = 128
SUBLANES = 8
BF16_ROWS = 2 * SUBLANES
MXU_DIM = 256
VMEM_LIMIT_BYTES = 56 * 1024 * 1024

TOKEN_TILE = 1024
POST_TILE = 512
HALO = BF16_ROWS
FILTER_TILE = 512
CONV_BLOCK = MXU_DIM
CONV_SUB = CONV_BLOCK // 2
SKEW_CHUNK = 1024
CONV_CH = 8
ATT_Q_TILE = 512
ATT_KV_TILE = 2048
V_EXT = HEAD_DIM + BF16_ROWS
SCORE_BOUND_MAX = 40.0


def _params(*sem):
    return pltpu.CompilerParams(dimension_semantics=sem, vmem_limit_bytes=VMEM_LIMIT_BYTES)


def _const_spec(shape):
    nd = len(shape)
    return pl.BlockSpec(shape, lambda *_: (0,) * nd, pipeline_mode=pl.Buffered(1))


def _rope_kernel(inv_ref, isrow_ref, cos_ref, sin_ref):
    tl = cos_ref.shape[0]
    t = pl.program_id(0) * tl + lax.broadcasted_iota(jnp.int32, (tl, 1), 0)
    row = (t // GRID_W).astype(F32)
    col = (t % GRID_W).astype(F32)
    ang = jnp.where(isrow_ref[...] > 0.5, row, col) * inv_ref[...]
    lane = lax.broadcasted_iota(jnp.int32, (1, LANES), 1)
    first_half = (lane % HEAD_DIM) < (HEAD_DIM // 2)
    sin = jnp.sin(ang)
    cos_ref[...] = jnp.cos(ang)
    sin_ref[...] = jnp.where(first_half, -sin, sin)


def _rope_tables(seq_len, inv_lane, isrow_lane):
    tl = TOKEN_TILE
    return pl.pallas_call(
        _rope_kernel,
        grid=(seq_len // tl,),
        in_specs=[_const_spec((1, LANES)), _const_spec((1, LANES))],
        out_specs=[pl.BlockSpec((tl, LANES), lambda i: (i, 0))] * 2,
        out_shape=[jax.ShapeDtypeStruct((seq_len, LANES), F32)] * 2,
        compiler_params=_params("parallel"),
        name="rope_tables",
    )(inv_lane, isrow_lane)


def _inproj_kernel(x_ref, xp_ref, xn_ref, g_ref, why_ref, wqkv_ref, gqk_ref, mavg_ref, cos_ref,
                   sin_ref, cw_ref, cb_ref, zt_ref, x0_ref, q_ref, k_ref, v_ref, u_scr, *,
                   seq_len):
    tm = x_ref.shape[0]
    t0 = (pl.program_id(0) * tm) % seq_len

    def normed(x):
        ms = jnp.mean(x * x, axis=-1, keepdims=True)
        return (x * lax.rsqrt(ms + EPS) * g_ref[...]).astype(BF16)

    h = normed(x_ref[...])
    qkv = jnp.dot(h, wqkv_ref[...], preferred_element_type=F32)

    qk = qkv[:, :QK_COLS]
    msq = jnp.dot((qk * qk).astype(BF16), mavg_ref[...], preferred_element_type=F32)

    has_prev = (t0 > 0).astype(F32)
    has_next = (t0 + tm < seq_len).astype(F32)
    h_ext = jnp.concatenate([normed(xp_ref[...] * has_prev), h, normed(xn_ref[...] * has_next)],
                            axis=0)
    u_scr[...] = jnp.dot(h_ext, why_ref[...], preferred_element_type=F32)
    cw = cw_ref[...]
    uc = (cb_ref[...] + u_scr[HALO - 1:HALO - 1 + tm, :] * cw[0:1]
          + u_scr[HALO:HALO + tm, :] * cw[1:2] + u_scr[HALO + 1:HALO + 1 + tm, :] * cw[2:3])
    x0_ref[...] = uc[:, :HY_WIDTH]
    zt_ref[...] = (uc[:, HY_WIDTH:2 * HY_WIDTH] * uc[:, 2 * HY_WIDTH:]).T

    qkn = qk * lax.rsqrt(msq + EPS) * gqk_ref[...]

    cos = cos_ref[...]
    sin_signed = sin_ref[...]
    lane = lax.broadcasted_iota(jnp.int32, (1, LANES), 1)
    first_half = (lane % HEAD_DIM) < (HEAD_DIM // 2)

    def rope(xg):
        partner = jnp.where(first_half, pltpu.roll(xg, LANES - HEAD_DIM // 2, 1),
                            pltpu.roll(xg, HEAD_DIM // 2, 1))
        return xg * cos + partner * sin_signed

    for j in range(Q_COLS // LANES):
        r = rope(qkn[:, j * LANES:(j + 1) * LANES])
        q_ref[j * LANES:(j + 1) * LANES, :] = r.T.astype(BF16)
    r = rope(qkn[:, Q_COLS:QK_COLS]).astype(BF16)
    k_ref[0] = r[:, :HEAD_DIM]
    k_ref[1] = r[:, HEAD_DIM:]
    vt = qkv[:, QK_COLS:].T.astype(BF16)
    ones = jnp.ones((V_EXT - HEAD_DIM, tm), BF16)
    for g in range(N_KV_HEADS):
        v_ref[g * V_EXT:g * V_EXT + HEAD_DIM, :] = vt[g * HEAD_DIM:(g + 1) * HEAD_DIM, :]
        v_ref[g * V_EXT + HEAD_DIM:(g + 1) * V_EXT, :] = ones


def _inproj(x2, g, w_hy, w_qkv, gqk, mavg, inv_lane, isrow_lane, conv_w, conv_b, seq_len):
    n = x2.shape[0]
    tm = TOKEN_TILE
    halo_per_tile = tm // HALO
    last_halo = n // HALO - 1
    tiles_per_seq = seq_len // tm
    cos, sin_signed = _rope_tables(seq_len, inv_lane, isrow_lane)
    return pl.pallas_call(
        functools.partial(_inproj_kernel, seq_len=seq_len),
        grid=(n // tm,),
        in_specs=[
            pl.BlockSpec((tm, D_MODEL), lambda i: (i, 0)),
            pl.BlockSpec((HALO, D_MODEL), lambda i: (jnp.maximum(i * halo_per_tile - 1, 0), 0)),
            pl.BlockSpec((HALO, D_MODEL),
                         lambda i: (jnp.minimum((i + 1) * halo_per_tile, last_halo), 0)),
            _const_spec((1, D_MODEL)),
            _const_spec((D_MODEL, HY_COLS)),
            _const_spec((D_MODEL, QK_COLS + KV_COLS)),
            _const_spec((1, QK_COLS)),
            _const_spec((QK_COLS, QK_COLS)),
            pl.BlockSpec((tm, LANES), lambda i: (i % tiles_per_seq, 0)),
            pl.BlockSpec((tm, LANES), lambda i: (i % tiles_per_seq, 0)),
            _const_spec((SHORT_CONV, HY_COLS)),
            _const_spec((1, HY_COLS)),
        ],
        out_specs=[
            pl.BlockSpec((HY_WIDTH, tm), lambda i: (0, i)),
            pl.BlockSpec((tm, HY_WIDTH), lambda i: (i, 0)),
            pl.BlockSpec((Q_COLS, tm), lambda i: (0, i)),
            pl.BlockSpec((N_KV_HEADS, tm, HEAD_DIM), lambda i: (0, i, 0)),
            pl.BlockSpec((N_KV_HEADS * V_EXT, tm), lambda i: (0, i)),
        ],
        out_shape=[
            jax.ShapeDtypeStruct((HY_WIDTH, n), F32),
            jax.ShapeDtypeStruct((n, HY_WIDTH), F32),
            jax.ShapeDtypeStruct((Q_COLS, n), BF16),
            jax.ShapeDtypeStruct((N_KV_HEADS, n, HEAD_DIM), BF16),
            jax.ShapeDtypeStruct((N_KV_HEADS * V_EXT, n), BF16),
        ],
        scratch_shapes=[pltpu.VMEM((tm + 2 * HALO, HY_COLS), F32)],
        compiler_params=_params("parallel"),
        name="inproj",
    )(x2, x2, x2, g, w_hy, w_qkv, gqk, mavg, cos, sin_signed, conv_w, conv_b)


def _filter_kernel(w1t_ref, w1c_ref, w1s_ref, b1_ref, w2_ref, b2_ref, fr_ref, bands_ref,
                   w3_ref, decay_ref, k_ref, ksum_ref, *, seq_len):
    tl = k_ref.shape[1]
    hp = lax.Precision.HIGHEST
    p = pl.program_id(0) * tl + lax.broadcasted_iota(jnp.int32, (1, tl), 1)
    n = jnp.abs(p - seq_len).astype(F32)
    t = n * (1.0 / (seq_len - 1))
    w = n * (2.0 * math.pi / seq_len)
    ang = bands_ref[...] * w
    pre = (w1t_ref[...] * t
           + jnp.dot(w1c_ref[...], jnp.cos(ang), precision=hp, preferred_element_type=F32)
           - jnp.dot(w1s_ref[...], jnp.sin(ang), precision=hp, preferred_element_type=F32)
           + b1_ref[...])
    fr = fr_ref[...]
    h = jnp.sin(fr * pre)
    h = jnp.sin(fr * (jnp.dot(w2_ref[...], h, precision=hp, preferred_element_type=F32)
                      + b2_ref[...]))
    h = jnp.dot(w3_ref[...], h, precision=hp, preferred_element_type=F32)
    h = h * jnp.exp(-t * jnp.abs(decay_ref[...]))
    h = jnp.where(p == 0, 0.0, h)
    k_ref[...] = h

    @pl.when(pl.program_id(0) == 0)
    def _():
        ksum_ref[...] = jnp.zeros_like(ksum_ref)

    ksum_ref[...] += jnp.broadcast_to(jnp.sum(jnp.abs(h), axis=1, keepdims=True), ksum_ref.shape)


def _hyena_filter(seq_len, w1, b1, w2, b2, w3, freq, decay):
    tl = FILTER_TILE
    n_tiles = 2 * seq_len // tl
    half = seq_len // tl
    col = lambda a: a.reshape(-1, 1)
    bands = jnp.linspace(1e-4, FILTER_BANDS - 1, FILTER_BANDS, dtype=F32).reshape(-1, 1)
    w1t = w1.T
    args = (w1t[:, :1], w1t[:, 1:1 + FILTER_BANDS], w1t[:, 1 + FILTER_BANDS:], col(b1),
            w2.T, col(b2), col(freq), jnp.asarray(bands),
            w3.T, decay.reshape(-1, 1))
    dir_map = lambda i: (jnp.where(i >= half, 0, 1), 0)
    in_specs = [_const_spec(a.shape) for a in args[:8]]
    in_specs += [pl.BlockSpec((HY_WIDTH, FILTER_HIDDEN), dir_map),
                 pl.BlockSpec((HY_WIDTH, 1), dir_map)]
    return pl.pallas_call(
        functools.partial(_filter_kernel, seq_len=seq_len),
        grid=(n_tiles,),
        in_specs=in_specs,
        out_specs=[pl.BlockSpec((HY_WIDTH, tl), lambda i: (0, i)),
                   pl.BlockSpec((HY_WIDTH, LANES), lambda i: (0, 0))],
        out_shape=[jax.ShapeDtypeStruct((HY_WIDTH, 2 * seq_len), F32),
                   jax.ShapeDtypeStruct((HY_WIDTH, LANES), F32)],
        compiler_params=_params("arbitrary"),
        name="hyena_filter",
    )(*args)


def _longconv_kernel(z_ref, k_ref, ksum_ref, d_ref, y_ref, zrows2, acc2, tsub2):
    ncb, bsz, seq_len = z_ref.shape
    p = CONV_BLOCK
    q = CONV_SUB
    nb = seq_len // p

    def channel(c, slot):
        zrows, acc, tsub = zrows2.at[slot], acc2.at[slot], tsub2.at[slot]
        zc = z_ref[c]
        for i in range(nb):
            zrows[i * bsz:(i + 1) * bsz, :] = zc[:, i * p:(i + 1) * p]
        acc[...] = jnp.zeros_like(acc)
        inv = 1.0 / ksum_ref[pl.ds(c, 1), :][:, :1]
        for lo in range(0, 2 * seq_len, SKEW_CHUNK):
            src_lo = max(lo - q, 0)
            kn = k_ref[pl.ds(c, 1), src_lo:lo + SKEW_CHUNK] * inv
            skew = pltpu.roll(jnp.broadcast_to(kn, (q, kn.shape[1])), 0, 1,
                              stride=1, stride_axis=0)
            tsub[:, lo:lo + SKEW_CHUNK] = skew[:, lo - src_lo:].astype(BF16)

        def sub_tile(m):
            return tsub[:, seq_len + m * q:seq_len + (m + 1) * q]

        for d in range(-(nb - 1), nb):
            t_mid = sub_tile(2 * d)
            w_d = jnp.concatenate(
                [jnp.concatenate([t_mid, sub_tile(2 * d + 1)], axis=1),
                 jnp.concatenate([sub_tile(2 * d - 1), t_mid], axis=1)], axis=0)
            j0, j1 = max(0, -d), min(nb, nb - d)
            lhs = zrows[j0 * bsz:j1 * bsz, :].astype(BF16)
            acc[(j0 + d) * bsz:(j1 + d) * bsz, :] += jnp.dot(lhs, w_d, preferred_element_type=F32)
        dc = d_ref[pl.ds(c, 1), :][:, :1]
        for i in range(nb):
            y_ref[c, :, i * p:(i + 1) * p] = (acc[i * bsz:(i + 1) * bsz, :]
                                              + zc[:, i * p:(i + 1) * p] * dc)

    def channel_pair(i, carry):
        channel(2 * i, 0)
        channel(2 * i + 1, 1)
        return carry

    lax.fori_loop(0, ncb // 2, channel_pair, 0)


def _longconv(zt3, kext, ksum, d_bias):
    c, bsz, seq_len = zt3.shape
    cb = CONV_CH
    nb = seq_len // CONV_BLOCK
    return pl.pallas_call(
        _longconv_kernel,
        grid=(c // cb,),
        in_specs=[pl.BlockSpec((cb, bsz, seq_len), lambda i: (i, 0, 0)),
                  pl.BlockSpec((cb, 2 * seq_len), lambda i: (i, 0)),
                  pl.BlockSpec((cb, LANES), lambda i: (i, 0)),
                  pl.BlockSpec((cb, LANES), lambda i: (i, 0))],
        out_specs=pl.BlockSpec((cb, bsz, seq_len), lambda i: (i, 0, 0)),
        out_shape=jax.ShapeDtypeStruct((c, bsz, seq_len), F32),
        scratch_shapes=[pltpu.VMEM((2, nb * bsz, CONV_BLOCK), F32),
                        pltpu.VMEM((2, nb * bsz, CONV_BLOCK), F32),
                        pltpu.VMEM((2, CONV_SUB, 2 * seq_len), BF16)],
        compiler_params=_params("parallel"),
        name="hyena_longconv",
    )(zt3, kext, ksum, d_bias)


def _attn_kernel(q_ref, k_ref, v_ref, o_ref, m_ref, acc_ref, *, kv_tile):
    seq_len = k_ref.shape[1]
    m_ref[...] = jnp.full(m_ref.shape, -jnp.inf, F32)
    acc_ref[...] = jnp.zeros(acc_ref.shape, F32)

    def step(c, carry):
        off = pl.multiple_of(c * kv_tile, kv_tile)
        kc = k_ref[0, pl.ds(off, kv_tile), :]
        vc = v_ref[:, pl.ds(off, kv_tile)]
        for h in range(GROUP):
            qh = q_ref[h * HEAD_DIM:(h + 1) * HEAD_DIM, :]
            s = jnp.dot(kc, qh, preferred_element_type=F32)
            m_old = m_ref[h]
            m_new = jnp.maximum(m_old, jnp.max(s, axis=0, keepdims=True))
            alpha = jnp.exp(m_old - m_new)
            p = jnp.exp(s - m_new).astype(BF16)
            acc_ref[h] = alpha * acc_ref[h] + jnp.dot(vc, p, preferred_element_type=F32)
            m_ref[h] = m_new
        return carry

    lax.fori_loop(0, seq_len // kv_tile, step, 0)
    for h in range(GROUP):
        a = acc_ref[h]
        o_ref[h * HEAD_DIM:(h + 1) * HEAD_DIM, :] = a[:HEAD_DIM] / a[HEAD_DIM:HEAD_DIM + 1]


def _attn_bounded_kernel(q_ref, k_ref, v_ref, o_ref, qcat_ref, acc_ref, *, kv_tile):
    seq_len = k_ref.shape[1]
    tq = q_ref.shape[1]
    acc_ref[...] = jnp.zeros(acc_ref.shape, F32)
    for h in range(GROUP):
        qcat_ref[:, h * tq:(h + 1) * tq] = q_ref[h * HEAD_DIM:(h + 1) * HEAD_DIM, :]

    def step(c, carry):
        off = pl.multiple_of(c * kv_tile, kv_tile)
        kc = k_ref[0, pl.ds(off, kv_tile), :]
        vc = v_ref[:, pl.ds(off, kv_tile)]
        p = jnp.exp(jnp.dot(kc, qcat_ref[...], preferred_element_type=F32)).astype(BF16)
        acc_ref[...] += jnp.dot(vc, p, preferred_element_type=F32)
        return carry

    lax.fori_loop(0, seq_len // kv_tile, step, 0)
    for h in range(GROUP):
        a = acc_ref[:, h * tq:(h + 1) * tq]
        o_ref[h * HEAD_DIM:(h + 1) * HEAD_DIM, :] = a[:HEAD_DIM] / a[HEAD_DIM:HEAD_DIM + 1]


def _attention(qt, k, vt, bsz, seq_len, score_bound):
    n = bsz * seq_len
    tq = ATT_Q_TILE
    nq = seq_len // tq
    kv_tile = min(ATT_KV_TILE, seq_len)
    gw = GROUP * HEAD_DIM

    def call(body, scratch, name):
        return pl.pallas_call(
            functools.partial(body, kv_tile=kv_tile),
            grid=(bsz, N_KV_HEADS, nq),
            in_specs=[pl.BlockSpec((gw, tq), lambda b, h, i: (h, b * nq + i)),
                      pl.BlockSpec((1, seq_len, HEAD_DIM), lambda b, h, i: (h, b, 0)),
                      pl.BlockSpec((V_EXT, seq_len), lambda b, h, i: (h, b))],
            out_specs=pl.BlockSpec((gw, tq), lambda b, h, i: (h, b * nq + i)),
            out_shape=jax.ShapeDtypeStruct((ATT_WIDTH, n), F32),
            scratch_shapes=scratch,
            compiler_params=_params("parallel", "parallel", "parallel"),
            name=name,
        )

    acc = pltpu.VMEM((GROUP, V_EXT, tq), F32)
    bounded = call(_attn_bounded_kernel,
                   [pltpu.VMEM((HEAD_DIM, GROUP * tq), BF16), pltpu.VMEM((V_EXT, GROUP * tq), F32)],
                   "attention_bounded")
    online = call(_attn_kernel, [pltpu.VMEM((GROUP, 1, tq), F32), acc], "attention_online")
    return lax.cond(score_bound <= SCORE_BOUND_MAX, bounded, online, qt, k, vt)


def _rms(x, g):
    return x * lax.rsqrt(jnp.mean(x * x, axis=-1, keepdims=True) + EPS) * g


def _post_kernel(x_ref, yt_ref, x0_ref, ya_ref, ghy_ref, gatt_ref, wout_ref, gffn_ref,
                 wg_ref, wu_ref, wd_ref, gfin_ref, o_ref):
    hm = x_ref.shape[0] // 2
    halves = [slice(r * hm, (r + 1) * hm) for r in range(2)]

    def mixer_out(sl):
        y_hy = x0_ref[sl, :] * yt_ref[:, sl].T
        return jnp.concatenate([_rms(y_hy, ghy_ref[...]), _rms(ya_ref[:, sl].T, gatt_ref[...])],
                               axis=-1).astype(BF16)

    def swiglu(gate, up):
        return (gate * jax.nn.sigmoid(gate) * up).astype(BF16)

    def dot(a, w_ref):
        return jnp.dot(a, w_ref[...], preferred_element_type=F32)

    mixed = [mixer_out(sl) for sl in halves]
    x1 = [x_ref[sl, :] + dot(m, wout_ref) for sl, m in zip(halves, mixed)]
    h = [_rms(x, gffn_ref[...]).astype(BF16) for x in x1]
    gate_up = [(dot(hh, wg_ref), dot(hh, wu_ref)) for hh in h]
    ffn = [dot(swiglu(g, u), wd_ref) for g, u in gate_up]
    for sl, x, f in zip(halves, x1, ffn):
        o_ref[sl, :] = _rms(x + f, gfin_ref[...])


def _post(x2, yt, x0c, y_att, g_hy, g_att, w_out, g_ffn, w_gate, w_up, w_down, g_fin):
    n = x2.shape[0]
    tm = POST_TILE
    return pl.pallas_call(
        _post_kernel,
        grid=(n // tm,),
        in_specs=[
            pl.BlockSpec((tm, D_MODEL), lambda i: (i, 0)),
            pl.BlockSpec((HY_WIDTH, tm), lambda i: (0, i)),
            pl.BlockSpec((tm, HY_WIDTH), lambda i: (i, 0)),
            pl.BlockSpec((ATT_WIDTH, tm), lambda i: (0, i)),
            _const_spec((1, HY_WIDTH)),
            _const_spec((1, ATT_WIDTH)),
            _const_spec((D_MODEL, D_MODEL)),
            _const_spec((1, D_MODEL)),
            _const_spec((D_MODEL, D_FF)),
            _const_spec((D_MODEL, D_FF)),
            _const_spec((D_FF, D_MODEL)),
            _const_spec((1, D_MODEL)),
        ],
        out_specs=pl.BlockSpec((tm, D_MODEL), lambda i: (i, 0)),
        out_shape=jax.ShapeDtypeStruct((n, D_MODEL), F32),
        compiler_params=_params("parallel"),
        name="post",
    )(x2, yt, x0c, y_att, g_hy, g_att, w_out, g_ffn, w_gate, w_up, w_down, g_fin)


def _rope_lane_constants():
    lane = np.arange(LANES)
    pair = lane % (HEAD_DIM // 2)
    inv = ROPE_THETA ** (-jnp.arange(0, AXIS_DIM, 2, dtype=F32) / AXIS_DIM)
    isrow = (pair < AXIS_DIM // 2).astype(np.float32)
    return inv[pair % (AXIS_DIM // 2)].reshape(1, LANES), jnp.asarray(isrow.reshape(1, LANES))


def _prepare_weights(norm_mix_g, w_in, hy_conv_w, hy_conv_b, hy_d, q_norm_g, k_norm_g,
                     hy_out_g, att_out_g, w_out, norm_ffn_g, w_gate, w_up, w_down, final_norm_g):
    row = lambda a: a.reshape(1, -1).astype(F32)
    perm = np.concatenate([np.arange(0, HEAD_DIM, 2), np.arange(1, HEAD_DIM, 2)])
    qk_perm = np.concatenate([h * HEAD_DIM + perm for h in range(N_HEADS + N_KV_HEADS)])
    w_hy = w_in[:, :HY_COLS].astype(BF16)
    w_qk = w_in[:, HY_COLS:HY_COLS + QK_COLS][:, qk_perm]
    w_qkv = jnp.concatenate([w_qk, w_in[:, HY_COLS + QK_COLS:]], axis=1).astype(BF16)
    scale = HEAD_DIM ** -0.5
    gqk = jnp.concatenate([jnp.tile(q_norm_g[perm] * scale, N_HEADS),
                           jnp.tile(k_norm_g[perm], N_KV_HEADS)]).reshape(1, QK_COLS)
    head_id = np.arange(QK_COLS) // HEAD_DIM
    mavg = jnp.asarray((head_id[:, None] == head_id[None, :]).astype(np.float32) / HEAD_DIM,
                       dtype=BF16)
    inv_lane, isrow_lane = _rope_lane_constants()
    d_bias = jnp.broadcast_to(hy_d.reshape(HY_WIDTH, 1), (HY_WIDTH, LANES))
    score_bound = (HEAD_DIM ** 0.5) * jnp.max(jnp.abs(q_norm_g)) * jnp.max(jnp.abs(k_norm_g))
    return dict(
        score_bound=score_bound,
        inproj=(row(norm_mix_g), w_hy, w_qkv, gqk, mavg, inv_lane, isrow_lane, hy_conv_w,
                row(hy_conv_b)),
        d_bias=d_bias,
        post=(row(hy_out_g), row(att_out_g), w_out.astype(BF16), row(norm_ffn_g),
              w_gate.astype(BF16), w_up.astype(BF16), w_down.astype(BF16), row(final_norm_g)),
    )


def _trunk(x, wts, filt_params):
    bsz, seq_len, d_model = x.shape
    assert d_model == D_MODEL and x.dtype == F32
    assert seq_len % TOKEN_TILE == 0 and seq_len % ATT_Q_TILE == 0 and seq_len % GRID_W == 0
    assert seq_len % min(ATT_KV_TILE, seq_len) == 0 and 2 * seq_len % SKEW_CHUNK == 0
    assert seq_len % FILTER_TILE == 0 and seq_len % CONV_BLOCK == 0
    assert HY_WIDTH % CONV_CH == 0 and CONV_CH % 2 == 0
    assert bsz % SUBLANES == 0
    n = bsz * seq_len
    assert n % POST_TILE == 0
    x2 = x.reshape(n, D_MODEL)
    zt, x0c, q, k, v = _inproj(x2, *wts["inproj"], seq_len)
    kext, ksum = _hyena_filter(seq_len, *filt_params)
    yt = _longconv(zt.reshape(HY_WIDTH, bsz, seq_len), kext, ksum, wts["d_bias"])
    y_att = _attention(q, k, v, bsz, seq_len, wts["score_bound"])
    out = _post(x2, yt.reshape(HY_WIDTH, n), x0c, y_att, *wts["post"])
    return out.reshape(bsz, seq_len, D_MODEL)


def kernel(x_prompt, x_sample, norm_mix_g, w_in, hy_conv_w, hy_conv_b, hy_f_w1, hy_f_b1, hy_f_w2,
           hy_f_b2, hy_f_w3, hy_f_freq, hy_decay, hy_d, q_norm_g, k_norm_g, hy_out_g, att_out_g,
           w_out, norm_ffn_g, w_gate, w_up, w_down, final_norm_g):
    wts = _prepare_weights(norm_mix_g[0], w_in[0], hy_conv_w[0], hy_conv_b[0], hy_d[0],
                           q_norm_g[0], k_norm_g[0], hy_out_g[0], att_out_g[0], w_out[0],
                           norm_ffn_g[0], w_gate[0], w_up[0], w_down[0], final_norm_g)
    filt_params = (hy_f_w1[0], hy_f_b1[0], hy_f_w2[0], hy_f_b2[0], hy_f_w3[0], hy_f_freq[0],
                   hy_decay[0])
    return (_trunk(x_prompt, wts, filt_params), _trunk(x_sample, wts, filt_params))
```

```python
import functools
import math

import jax
import jax.numpy as jnp
import numpy as np
from jax import lax
from jax.experimental import pallas as pl
from jax.experimental.pallas import tpu as pltpu

F32 = jnp.float32
BF16 = jnp.bfloat16

D_MODEL = 1024
HY_WIDTH = 512
ATT_WIDTH = 512
HEAD_DIM = 64
N_HEADS = 8
N_KV_HEADS = 2
GROUP = N_HEADS // N_KV_HEADS
HY_COLS = 3 * HY_WIDTH
Q_COLS = N_HEADS * HEAD_DIM
KV_COLS = N_KV_HEADS * HEAD_DIM
QK_COLS = Q_COLS + KV_COLS
SHORT_CONV = 3
FILTER_EMB = 33
FILTER_BANDS = 16
FILTER_HIDDEN = 64
GRID_W = 64
ROPE_THETA = 10000.0
AXIS_DIM = HEAD_DIM // 2
D_FF = 2816
EPS = 1e-6

LANES = 128
SUBLANES = 8
BF16_ROWS = 2 * SUBLANES
MXU_DIM = 256
VMEM_LIMIT_BYTES = 56 * 1024 * 1024

TOKEN_TILE = 1024
POST_TILE = 512
HALO = BF16_ROWS
FILTER_TILE = 2048
CONV_BLOCK = MXU_DIM
CONV_SUB = CONV_BLOCK // 2
SKEW_CHUNK = 1024
CONV_CH = 8
ATT_Q_TILE = 512
ATT_KV_TILE = 2048
V_EXT = HEAD_DIM + BF16_ROWS
SCORE_BOUND_MAX = 40.0


def _params(*sem):
    return pltpu.CompilerParams(dimension_semantics=sem, vmem_limit_bytes=VMEM_LIMIT_BYTES)


def _const_spec(shape):
    nd = len(shape)
    return pl.BlockSpec(shape, lambda *_: (0,) * nd, pipeline_mode=pl.Buffered(1))


def _rope_kernel(inv_ref, isrow_ref, cos_ref, sin_ref):
    tl = cos_ref.shape[0]
    t = pl.program_id(0) * tl + lax.broadcasted_iota(jnp.int32, (tl, 1), 0)
    row = (t // GRID_W).astype(F32)
    col = (t % GRID_W).astype(F32)
    ang = jnp.where(isrow_ref[...] > 0.5, row, col) * inv_ref[...]
    lane = lax.broadcasted_iota(jnp.int32, (1, LANES), 1)
    first_half = (lane % HEAD_DIM) < (HEAD_DIM // 2)
    sin = jnp.sin(ang)
    cos_ref[...] = jnp.cos(ang)
    sin_ref[...] = jnp.where(first_half, -sin, sin)


def _rope_tables(seq_len, inv_lane, isrow_lane):
    tl = TOKEN_TILE
    return pl.pallas_call(
        _rope_kernel,
        grid=(seq_len // tl,),
        in_specs=[_const_spec((1, LANES)), _const_spec((1, LANES))],
        out_specs=[pl.BlockSpec((tl, LANES), lambda i: (i, 0))] * 2,
        out_shape=[jax.ShapeDtypeStruct((seq_len, LANES), F32)] * 2,
        compiler_params=_params("parallel"),
        name="rope_tables",
    )(inv_lane, isrow_lane)


def _inproj_kernel(x_ref, xp_ref, xn_ref, g_ref, why_ref, wqkv_ref, gqk_ref, mavg_ref, cos_ref,
                   sin_ref, cw_ref, cb_ref, zt_ref, x0_ref, q_ref, k_ref, v_ref, u_scr, *,
                   seq_len):
    tm = x_ref.shape[0]
    t0 = (pl.program_id(0) * tm) % seq_len

    def normed(x):
        ms = jnp.mean(x * x, axis=-1, keepdims=True)
        return (x * lax.rsqrt(ms + EPS) * g_ref[...]).astype(BF16)

    h = normed(x_ref[...])
    qkv = jnp.dot(h, wqkv_ref[...], preferred_element_type=F32)

    qk = qkv[:, :QK_COLS]
    msq = jnp.dot((qk * qk).astype(BF16), mavg_ref[...], preferred_element_type=F32)

    has_prev = (t0 > 0).astype(F32)
    has_next = (t0 + tm < seq_len).astype(F32)
    h_ext = jnp.concatenate([normed(xp_ref[...] * has_prev), h, normed(xn_ref[...] * has_next)],
                            axis=0)
    u_scr[...] = jnp.dot(h_ext, why_ref[...], preferred_element_type=F32)
    cw = cw_ref[...]
    uc = (cb_ref[...] + u_scr[HALO - 1:HALO - 1 + tm, :] * cw[0:1]
          + u_scr[HALO:HALO + tm, :] * cw[1:2] + u_scr[HALO + 1:HALO + 1 + tm, :] * cw[2:3])
    x0_ref[...] = uc[:, :HY_WIDTH]
    zt_ref[...] = (uc[:, HY_WIDTH:2 * HY_WIDTH] * uc[:, 2 * HY_WIDTH:]).T

    qkn = qk * lax.rsqrt(msq + EPS) * gqk_ref[...]

    cos = cos_ref[...]
    sin_signed = sin_ref[...]
    lane = lax.broadcasted_iota(jnp.int32, (1, LANES), 1)
    first_half = (lane % HEAD_DIM) < (HEAD_DIM // 2)

    def rope(xg):
        partner = jnp.where(first_half, pltpu.roll(xg, LANES - HEAD_DIM // 2, 1),
                            pltpu.roll(xg, HEAD_DIM // 2, 1))
        return xg * cos + partner * sin_signed

    for j in range(Q_COLS // LANES):
        r = rope(qkn[:, j * LANES:(j + 1) * LANES])
        q_ref[j * LANES:(j + 1) * LANES, :] = r.T.astype(BF16)
    r = rope(qkn[:, Q_COLS:QK_COLS]).astype(BF16)
    k_ref[0] = r[:, :HEAD_DIM]
    k_ref[1] = r[:, HEAD_DIM:]
    vt = qkv[:, QK_COLS:].T.astype(BF16)
    ones = jnp.ones((V_EXT - HEAD_DIM, tm), BF16)
    for g in range(N_KV_HEADS):
        v_ref[g * V_EXT:g * V_EXT + HEAD_DIM, :] = vt[g * HEAD_DIM:(g + 1) * HEAD_DIM, :]
        v_ref[g * V_EXT + HEAD_DIM:(g + 1) * V_EXT, :] = ones


def _inproj(x2, g, w_hy, w_qkv, gqk, mavg, inv_lane, isrow_lane, conv_w, conv_b, seq_len):
    n = x2.shape[0]
    tm = TOKEN_TILE
    halo_per_tile = tm // HALO
    last_halo = n // HALO - 1
    tiles_per_seq = seq_len // tm
    cos, sin_signed = _rope_tables(seq_len, inv_lane, isrow_lane)
    return pl.pallas_call(
        functools.partial(_inproj_kernel, seq_len=seq_len),
        grid=(n // tm,),
        in_specs=[
            pl.BlockSpec((tm, D_MODEL), lambda i: (i, 0)),
            pl.BlockSpec((HALO, D_MODEL), lambda i: (jnp.maximum(i * halo_per_tile - 1, 0), 0)),
            pl.BlockSpec((HALO, D_MODEL),
                         lambda i: (jnp.minimum((i + 1) * halo_per_tile, last_halo), 0)),
            _const_spec((1, D_MODEL)),
            _const_spec((D_MODEL, HY_COLS)),
            _const_spec((D_MODEL, QK_COLS + KV_COLS)),
            _const_spec((1, QK_COLS)),
            _const_spec((QK_COLS, QK_COLS)),
            pl.BlockSpec((tm, LANES), lambda i: (i % tiles_per_seq, 0)),
            pl.BlockSpec((tm, LANES), lambda i: (i % tiles_per_seq, 0)),
            _const_spec((SHORT_CONV, HY_COLS)),
            _const_spec((1, HY_COLS)),
        ],
        out_specs=[
            pl.BlockSpec((HY_WIDTH, tm), lambda i: (0, i)),
            pl.BlockSpec((tm, HY_WIDTH), lambda i: (i, 0)),
            pl.BlockSpec((Q_COLS, tm), lambda i: (0, i)),
            pl.BlockSpec((N_KV_HEADS, tm, HEAD_DIM), lambda i: (0, i, 0)),
            pl.BlockSpec((N_KV_HEADS * V_EXT, tm), lambda i: (0, i)),
        ],
        out_shape=[
            jax.ShapeDtypeStruct((HY_WIDTH, n), F32),
            jax.ShapeDtypeStruct((n, HY_WIDTH), F32),
            jax.ShapeDtypeStruct((Q_COLS, n), BF16),
            jax.ShapeDtypeStruct((N_KV_HEADS, n, HEAD_DIM), BF16),
            jax.ShapeDtypeStruct((N_KV_HEADS * V_EXT, n), BF16),
        ],
        scratch_shapes=[pltpu.VMEM((tm + 2 * HALO, HY_COLS), F32)],
        compiler_params=_params("parallel"),
        name="inproj",
    )(x2, x2, x2, g, w_hy, w_qkv, gqk, mavg, cos, sin_signed, conv_w, conv_b)


def _filter_kernel(w1t_ref, w1c_ref, w1s_ref, b1_ref, w2_ref, b2_ref, fr_ref, bands_ref,
                   w3_ref, decay_ref, k_ref, ksum_ref, *, seq_len):
    tl = k_ref.shape[1]
    hp = lax.Precision.HIGHEST
    p = pl.program_id(0) * tl + lax.broadcasted_iota(jnp.int32, (1, tl), 1)
    n = jnp.abs(p - seq_len).astype(F32)
    t = n * (1.0 / (seq_len - 1))
    w = n * (2.0 * math.pi / seq_len)
    ang = bands_ref[...] * w
    pre = (w1t_ref[...] * t
           + jnp.dot(w1c_ref[...], jnp.cos(ang), precision=hp, preferred_element_type=F32)
           - jnp.dot(w1s_ref[...], jnp.sin(ang), precision=hp, preferred_element_type=F32)
           + b1_ref[...])
    fr = fr_ref[...]
    h = jnp.sin(fr * pre)
    h = jnp.sin(fr * (jnp.dot(w2_ref[...], h, precision=hp, preferred_element_type=F32)
                      + b2_ref[...]))
    h = jnp.dot(w3_ref[...], h, precision=hp, preferred_element_type=F32)
    h = h * jnp.exp(-t * jnp.abs(decay_ref[...]))
    h = jnp.where(p == 0, 0.0, h)
    k_ref[...] = h

    @pl.when(pl.program_id(0) == 0)
    def _():
        ksum_ref[...] = jnp.zeros_like(ksum_ref)

    ksum_ref[...] += jnp.broadcast_to(jnp.sum(jnp.abs(h), axis=1, keepdims=True), ksum_ref.shape)


def _hyena_filter(seq_len, w1, b1, w2, b2, w3, freq, decay):
    tl = FILTER_TILE
    n_tiles = 2 * seq_len // tl
    half = seq_len // tl
    col = lambda a: a.reshape(-1, 1)
    bands = jnp.linspace(1e-4, FILTER_BANDS - 1, FILTER_BANDS, dtype=F32).reshape(-1, 1)
    w1t = w1.T
    args = (w1t[:, :1], w1t[:, 1:1 + FILTER_BANDS], w1t[:, 1 + FILTER_BANDS:], col(b1),
            w2.T, col(b2), col(freq), jnp.asarray(bands),
            w3.T, decay.reshape(-1, 1))
    dir_map = lambda i: (jnp.where(i >= half, 0, 1), 0)
    in_specs = [_const_spec(a.shape) for a in args[:8]]
    in_specs += [pl.BlockSpec((HY_WIDTH, FILTER_HIDDEN), dir_map),
                 pl.BlockSpec((HY_WIDTH, 1), dir_map)]
    return pl.pallas_call(
        functools.partial(_filter_kernel, seq_len=seq_len),
        grid=(n_tiles,),
        in_specs=in_specs,
        out_specs=[pl.BlockSpec((HY_WIDTH, tl), lambda i: (0, i)),
                   pl.BlockSpec((HY_WIDTH, LANES), lambda i: (0, 0))],
        out_shape=[jax.ShapeDtypeStruct((HY_WIDTH, 2 * seq_len), F32),
                   jax.ShapeDtypeStruct((HY_WIDTH, LANES), F32)],
        compiler_params=_params("arbitrary"),
        name="hyena_filter",
    )(*args)


def _longconv_kernel(z_ref, k_ref, ksum_ref, d_ref, y_ref, zrows2, acc2, tsub2):
    ncb, bsz, seq_len = z_ref.shape
    p = CONV_BLOCK
    q = CONV_SUB
    nb = seq_len // p

    def channel(c, slot):
        zrows, acc, tsub = zrows2.at[slot], acc2.at[slot], tsub2.at[slot]
        zc = z_ref[c]
        for i in range(nb):
            zrows[i * bsz:(i + 1) * bsz, :] = zc[:, i * p:(i + 1) * p]
        acc[...] = jnp.zeros_like(acc)
        inv = 1.0 / ksum_ref[pl.ds(c, 1), :][:, :1]
        for lo in range(0, 2 * seq_len, SKEW_CHUNK):
            src_lo = max(lo - q, 0)
            kn = k_ref[pl.ds(c, 1), src_lo:lo + SKEW_CHUNK] * inv
            skew = pltpu.roll(jnp.broadcast_to(kn, (q, kn.shape[1])), 0, 1,
                              stride=1, stride_axis=0)
            tsub[:, lo:lo + SKEW_CHUNK] = skew[:, lo - src_lo:].astype(BF16)

        def sub_tile(m):
            return tsub[:, seq_len + m * q:seq_len + (m + 1) * q]

        for d in range(-(nb - 1), nb):
            t_mid = sub_tile(2 * d)
            w_d = jnp.concatenate(
                [jnp.concatenate([t_mid, sub_tile(2 * d + 1)], axis=1),
                 jnp.concatenate([sub_tile(2 * d - 1), t_mid], axis=1)], axis=0)
            j0, j1 = max(0, -d), min(nb, nb - d)
            lhs = zrows[j0 * bsz:j1 * bsz, :].astype(BF16)
            acc[(j0 + d) * bsz:(j1 + d) * bsz, :] += jnp.dot(lhs, w_d, preferred_element_type=F32)
        dc = d_ref[pl.ds(c, 1), :][:, :1]
        for i in range(nb):
            y_ref[c, :, i * p:(i + 1) * p] = (acc[i * bsz:(i + 1) * bsz, :]
                                              + zc[:, i * p:(i + 1) * p] * dc)

    def channel_pair(i, carry):
        channel(2 * i, 0)
        channel(2 * i + 1, 1)
        return carry

    lax.fori_loop(0, ncb // 2, channel_pair, 0)


def _longconv(zt3, kext, ksum, d_bias):
    c, bsz, seq_len = zt3.shape
    cb = CONV_CH
    nb = seq_len // CONV_BLOCK
    return pl.pallas_call(
        _longconv_kernel,
        grid=(c // cb,),
        in_specs=[pl.BlockSpec((cb, bsz, seq_len), lambda i: (i, 0, 0)),
                  pl.BlockSpec((cb, 2 * seq_len), lambda i: (i, 0)),
                  pl.BlockSpec((cb, LANES), lambda i: (i, 0)),
                  pl.BlockSpec((cb, LANES), lambda i: (i, 0))],
        out_specs=pl.BlockSpec((cb, bsz, seq_len), lambda i: (i, 0, 0)),
        out_shape=jax.ShapeDtypeStruct((c, bsz, seq_len), F32),
        scratch_shapes=[pltpu.VMEM((2, nb * bsz, CONV_BLOCK), F32),
                        pltpu.VMEM((2, nb * bsz, CONV_BLOCK), F32),
                        pltpu.VMEM((2, CONV_SUB, 2 * seq_len), BF16)],
        compiler_params=_params("parallel"),
        name="hyena_longconv",
    )(zt3, kext, ksum, d_bias)


def _attn_kernel(q_ref, k_ref, v_ref, o_ref, m_ref, acc_ref, *, kv_tile):
    seq_len = k_ref.shape[1]
    m_ref[...] = jnp.full(m_ref.shape, -jnp.inf, F32)
    acc_ref[...] = jnp.zeros(acc_ref.shape, F32)

    def step(c, carry):
        off = pl.multiple_of(c * kv_tile, kv_tile)
        kc = k_ref[0, pl.ds(off, kv_tile), :]
        vc = v_ref[:, pl.ds(off, kv_tile)]
        for h in range(GROUP):
            qh = q_ref[h * HEAD_DIM:(h + 1) * HEAD_DIM, :]
            s = jnp.dot(kc, qh, preferred_element_type=F32)
            m_old = m_ref[h]
            m_new = jnp.maximum(m_old, jnp.max(s, axis=0, keepdims=True))
            alpha = jnp.exp(m_old - m_new)
            p = jnp.exp(s - m_new).astype(BF16)
            acc_ref[h] = alpha * acc_ref[h] + jnp.dot(vc, p, preferred_element_type=F32)
            m_ref[h] = m_new
        return carry

    lax.fori_loop(0, seq_len // kv_tile, step, 0)
    for h in range(GROUP):
        a = acc_ref[h]
        o_ref[h * HEAD_DIM:(h + 1) * HEAD_DIM, :] = a[:HEAD_DIM] / a[HEAD_DIM:HEAD_DIM + 1]


def _attn_bounded_kernel(q_ref, k_ref, v_ref, o_ref, qcat_ref, acc_ref, *, kv_tile):
    seq_len = k_ref.shape[1]
    tq = q_ref.shape[1]
    acc_ref[...] = jnp.zeros(acc_ref.shape, F32)
    for h in range(GROUP):
        qcat_ref[:, h * tq:(h + 1) * tq] = q_ref[h * HEAD_DIM:(h + 1) * HEAD_DIM, :]

    def step(c, carry):
        off = pl.multiple_of(c * kv_tile, kv_tile)
        kc = k_ref[0, pl.ds(off, kv_tile), :]
        vc = v_ref[:, pl.ds(off, kv_tile)]
        p = jnp.exp(jnp.dot(kc, qcat_ref[...], preferred_element_type=F32)).astype(BF16)
        acc_ref[...] += jnp.dot(vc, p, preferred_element_type=F32)
        return carry

    lax.fori_loop(0, seq_len // kv_tile, step, 0, unroll=True)
    for h in range(GROUP):
        a = acc_ref[:, h * tq:(h + 1) * tq]
        o_ref[h * HEAD_DIM:(h + 1) * HEAD_DIM, :] = a[:HEAD_DIM] / a[HEAD_DIM:HEAD_DIM + 1]


def _attention(qt, k, vt, bsz, seq_len, score_bound):
    n = bsz * seq_len
    tq = ATT_Q_TILE
    nq = seq_len // tq
    kv_tile = min(ATT_KV_TILE, seq_len)
    gw = GROUP * HEAD_DIM

    def call(body, scratch, name):
        return pl.pallas_call(
            functools.partial(body, kv_tile=kv_tile),
            grid=(bsz, N_KV_HEADS, nq),
            in_specs=[pl.BlockSpec((gw, tq), lambda b, h, i: (h, b * nq + i)),
                      pl.BlockSpec((1, seq_len, HEAD_DIM), lambda b, h, i: (h, b, 0)),
                      pl.BlockSpec((V_EXT, seq_len), lambda b, h, i: (h, b))],
            out_specs=pl.BlockSpec((gw, tq), lambda b, h, i: (h, b * nq + i)),
            out_shape=jax.ShapeDtypeStruct((ATT_WIDTH, n), F32),
            scratch_shapes=scratch,
            compiler_params=_params("parallel", "parallel", "parallel"),
            name=name,
        )

    acc = pltpu.VMEM((GROUP, V_EXT, tq), F32)
    bounded = call(_attn_bounded_kernel,
                   [pltpu.VMEM((HEAD_DIM, GROUP * tq), BF16), pltpu.VMEM((V_EXT, GROUP * tq), F32)],
                   "attention_bounded")
    online = call(_attn_kernel, [pltpu.VMEM((GROUP, 1, tq), F32), acc], "attention_online")
    return lax.cond(score_bound <= SCORE_BOUND_MAX, bounded, online, qt, k, vt)


def _rms(x, g):
    return x * lax.rsqrt(jnp.mean(x * x, axis=-1, keepdims=True) + EPS) * g


def _post_kernel(x_ref, yt_ref, x0_ref, ya_ref, ghy_ref, gatt_ref, wout_ref, gffn_ref,
                 wg_ref, wu_ref, wd_ref, gfin_ref, o_ref):
    hm = x_ref.shape[0] // 2
    halves = [slice(r * hm, (r + 1) * hm) for r in range(2)]

    def mixer_out(sl):
        y_hy = x0_ref[sl, :] * yt_ref[:, sl].T
        return jnp.concatenate([_rms(y_hy, ghy_ref[...]), _rms(ya_ref[:, sl].T, gatt_ref[...])],
                               axis=-1).astype(BF16)

    def swiglu(gate, up):
        return (gate * jax.nn.sigmoid(gate) * up).astype(BF16)

    def dot(a, w_ref):
        return jnp.dot(a, w_ref[...], preferred_element_type=F32)

    mixed = [mixer_out(sl) for sl in halves]
    x1 = [x_ref[sl, :] + dot(m, wout_ref) for sl, m in zip(halves, mixed)]
    h = [_rms(x, gffn_ref[...]).astype(BF16) for x in x1]
    gate_up = [(dot(hh, wg_ref), dot(hh, wu_ref)) for hh in h]
    ffn = [dot(swiglu(g, u), wd_ref) for g, u in gate_up]
    for sl, x, f in zip(halves, x1, ffn):
        o_ref[sl, :] = _rms(x + f, gfin_ref[...])


def _post(x2, yt, x0c, y_att, g_hy, g_att, w_out, g_ffn, w_gate, w_up, w_down, g_fin):
    n = x2.shape[0]
    tm = POST_TILE
    return pl.pallas_call(
        _post_kernel,
        grid=(n // tm,),
        in_specs=[
            pl.BlockSpec((tm, D_MODEL), lambda i: (i, 0)),
            pl.BlockSpec((HY_WIDTH, tm), lambda i: (0, i)),
            pl.BlockSpec((tm, HY_WIDTH), lambda i: (i, 0)),
            pl.BlockSpec((ATT_WIDTH, tm), lambda i: (0, i)),
            _const_spec((1, HY_WIDTH)),
            _const_spec((1, ATT_WIDTH)),
            _const_spec((D_MODEL, D_MODEL)),
            _const_spec((1, D_MODEL)),
            _const_spec((D_MODEL, D_FF)),
            _const_spec((D_MODEL, D_FF)),
            _const_spec((D_FF, D_MODEL)),
            _const_spec((1, D_MODEL)),
        ],
        out_specs=pl.BlockSpec((tm, D_MODEL), lambda i: (i, 0)),
        out_shape=jax.ShapeDtypeStruct((n, D_MODEL), F32),
        compiler_params=_params("parallel"),
        name="post",
    )(x2, yt, x0c, y_att, g_hy, g_att, w_out, g_ffn, w_gate, w_up, w_down, g_fin)


def _rope_lane_constants():
    lane = np.arange(LANES)
    pair = lane % (HEAD_DIM // 2)
    inv = ROPE_THETA ** (-jnp.arange(0, AXIS_DIM, 2, dtype=F32) / AXIS_DIM)
    isrow = (pair < AXIS_DIM // 2).astype(np.float32)
    return inv[pair % (AXIS_DIM // 2)].reshape(1, LANES), jnp.asarray(isrow.reshape(1, LANES))


def _prepare_weights(norm_mix_g, w_in, hy_conv_w, hy_conv_b, hy_d, q_norm_g, k_norm_g,
                     hy_out_g, att_out_g, w_out, norm_ffn_g, w_gate, w_up, w_down, final_norm_g):
    row = lambda a: a.reshape(1, -1).astype(F32)
    perm = np.concatenate([np.arange(0, HEAD_DIM, 2), np.arange(1, HEAD_DIM, 2)])
    qk_perm = np.concatenate([h * HEAD_DIM + perm for h in range(N_HEADS + N_KV_HEADS)])
    w_hy = w_in[:, :HY_COLS].astype(BF16)
    w_qk = w_in[:, HY_COLS:HY_COLS + QK_COLS][:, qk_perm]
    w_qkv = jnp.concatenate([w_qk, w_in[:, HY_COLS + QK_COLS:]], axis=1).astype(BF16)
    scale = HEAD_DIM ** -0.5
    gqk = jnp.concatenate([jnp.tile(q_norm_g[perm] * scale, N_HEADS),
                           jnp.tile(k_norm_g[perm], N_KV_HEADS)]).reshape(1, QK_COLS)
    head_id = np.arange(QK_COLS) // HEAD_DIM
    mavg = jnp.asarray((head_id[:, None] == head_id[None, :]).astype(np.float32) / HEAD_DIM,
                       dtype=BF16)
    inv_lane, isrow_lane = _rope_lane_constants()
    d_bias = jnp.broadcast_to(hy_d.reshape(HY_WIDTH, 1), (HY_WIDTH, LANES))
    score_bound = (HEAD_DIM ** 0.5) * jnp.max(jnp.abs(q_norm_g)) * jnp.max(jnp.abs(k_norm_g))
    return dict(
        score_bound=score_bound,
        inproj=(row(norm_mix_g), w_hy, w_qkv, gqk, mavg, inv_lane, isrow_lane, hy_conv_w,
                row(hy_conv_b)),
        d_bias=d_bias,
        post=(row(hy_out_g), row(att_out_g), w_out.astype(BF16), row(norm_ffn_g),
              w_gate.astype(BF16), w_up.astype(BF16), w_down.astype(BF16), row(final_norm_g)),
    )


def _trunk(x, wts, filt_params):
    bsz, seq_len, d_model = x.shape
    assert d_model == D_MODEL and x.dtype == F32
    assert seq_len % TOKEN_TILE == 0 and seq_len % ATT_Q_TILE == 0 and seq_len % GRID_W == 0
    assert seq_len % min(ATT_KV_TILE, seq_len) == 0 and 2 * seq_len % SKEW_CHUNK == 0
    assert seq_len % FILTER_TILE == 0 and seq_len % CONV_BLOCK == 0
    assert HY_WIDTH % CONV_CH == 0 and CONV_CH % 2 == 0
    assert bsz % SUBLANES == 0
    n = bsz * seq_len
    assert n % POST_TILE == 0
    x2 = x.reshape(n, D_MODEL)
    zt, x0c, q, k, v = _inproj(x2, *wts["inproj"], seq_len)
    kext, ksum = _hyena_filter(seq_len, *filt_params)
    yt = _longconv(zt.reshape(HY_WIDTH, bsz, seq_len), kext, ksum, wts["d_bias"])
    y_att = _attention(q, k, v, bsz, seq_len, wts["score_bound"])
    out = _post(x2, yt.reshape(HY_WIDTH, n), x0c, y_att, *wts["post"])
    return out.reshape(bsz, seq_len, D_MODEL)


def kernel(x_prompt, x_sample, norm_mix_g, w_in, hy_conv_w, hy_conv_b, hy_f_w1, hy_f_b1, hy_f_w2,
           hy_f_b2, hy_f_w3, hy_f_freq, hy_decay, hy_d, q_norm_g, k_norm_g, hy_out_g, att_out_g,
           w_out, norm_ffn_g, w_gate, w_up, w_down, final_norm_g):
    wts = _prepare_weights(norm_mix_g[0], w_in[0], hy_conv_w[0], hy_conv_b[0], hy_d[0],
                           q_norm_g[0], k_norm_g[0], hy_out_g[0], att_out_g[0], w_out[0],
                           norm_ffn_g[0], w_gate[0], w_up[0], w_down[0], final_norm_g)
    filt_params = (hy_f_w1[0], hy_f_b1[0], hy_f_w2[0], hy_f_b2[0], hy_f_w3[0], hy_f_freq[0],
                   hy_decay[0])
    return (_trunk(x_prompt, wts, filt_params), _trunk(x_sample, wts, filt_params))
```

```python
import functools
import math

import jax
import jax.numpy as jnp
import numpy as np
from jax import lax
from jax.experimental import pallas as pl
from jax.experimental.pallas import tpu as pltpu

F32 = jnp.float32
BF16 = jnp.bfloat16

D_MODEL = 1024
HY_WIDTH = 512
ATT_WIDTH = 512
HEAD_DIM = 64
N_HEADS = 8
N_KV_HEADS = 2
GROUP = N_HEADS // N_KV_HEADS
HY_COLS = 3 * HY_WIDTH
Q_COLS = N_HEADS * HEAD_DIM
KV_COLS = N_KV_HEADS * HEAD_DIM
QK_COLS = Q_COLS + KV_COLS
SHORT_CONV = 3
FILTER_EMB = 33
FILTER_BANDS = 16
FILTER_HIDDEN = 64
GRID_W = 64
ROPE_THETA = 10000.0
AXIS_DIM = HEAD_DIM // 2
D_FF = 2816
EPS = 1e-6

LANES = 128
SUBLANES = 8
BF16_ROWS = 2 * SUBLANES
MXU_DIM = 256
VMEM_LIMIT_BYTES = 56 * 1024 * 1024

TOKEN_TILE = 1024
POST_TILE = 512
HALO = BF16_ROWS
FILTER_TILE = 2048
CONV_BLOCK = MXU_DIM
CONV_SUB = CONV_BLOCK // 2
SKEW_CHUNK = 1024
CONV_CH = 8
ATT_Q_TILE = 512
ATT_KV_TILE = 2048
V_EXT = HEAD_DIM + BF16_ROWS
SCORE_BOUND_MAX = 40.0


def _params(*sem):
    return pltpu.CompilerParams(dimension_semantics=sem, vmem_limit_bytes=VMEM_LIMIT_BYTES)


def _const_spec(shape):
    nd = len(shape)
    return pl.BlockSpec(shape, lambda *_: (0,) * nd, pipeline_mode=pl.Buffered(1))


def _rope_kernel(inv_ref, isrow_ref, cos_ref, sin_ref):
    tl = cos_ref.shape[0]
    t = pl.program_id(0) * tl + lax.broadcasted_iota(jnp.int32, (tl, 1), 0)
    row = (t // GRID_W).astype(F32)
    col = (t % GRID_W).astype(F32)
    ang = jnp.where(isrow_ref[...] > 0.5, row, col) * inv_ref[...]
    lane = lax.broadcasted_iota(jnp.int32, (1, LANES), 1)
    first_half = (lane % HEAD_DIM) < (HEAD_DIM // 2)
    sin = jnp.sin(ang)
    cos_ref[...] = jnp.cos(ang)
    sin_ref[...] = jnp.where(first_half, -sin, sin)


def _rope_tables(seq_len, inv_lane, isrow_lane):
    tl = TOKEN_TILE
    return pl.pallas_call(
        _rope_kernel,
        grid=(seq_len // tl,),
        in_specs=[_const_spec((1, LANES)), _const_spec((1, LANES))],
        out_specs=[pl.BlockSpec((tl, LANES), lambda i: (i, 0))] * 2,
        out_shape=[jax.ShapeDtypeStruct((seq_len, LANES), F32)] * 2,
        compiler_params=_params("parallel"),
        name="rope_tables",
    )(inv_lane, isrow_lane)


def _inproj_kernel(x_ref, xp_ref, xn_ref, g_ref, why_ref, wqkv_ref, gqk_ref, mavg_ref, cos_ref,
                   sin_ref, cw_ref, cb_ref, zt_ref, x0_ref, q_ref, k_ref, v_ref, u_scr, *,
                   seq_len):
    tm = x_ref.shape[0]
    t0 = (pl.program_id(0) * tm) % seq_len

    def normed(x):
        ms = jnp.mean(x * x, axis=-1, keepdims=True)
        return (x * lax.rsqrt(ms + EPS) * g_ref[...]).astype(BF16)

    h = normed(x_ref[...])
    qkv = jnp.dot(h, wqkv_ref[...], preferred_element_type=F32)

    qk = qkv[:, :QK_COLS]
    msq = jnp.dot((qk * qk).astype(BF16), mavg_ref[...], preferred_element_type=F32)

    has_prev = (t0 > 0).astype(F32)
    has_next = (t0 + tm < seq_len).astype(F32)
    h_ext = jnp.concatenate([normed(xp_ref[...] * has_prev), h, normed(xn_ref[...] * has_next)],
                            axis=0)
    u_scr[...] = jnp.dot(h_ext, why_ref[...], preferred_element_type=F32)
    cw = cw_ref[...]
    uc = (cb_ref[...] + u_scr[HALO - 1:HALO - 1 + tm, :] * cw[0:1]
          + u_scr[HALO:HALO + tm, :] * cw[1:2] + u_scr[HALO + 1:HALO + 1 + tm, :] * cw[2:3])
    x0_ref[...] = uc[:, :HY_WIDTH]
    zt_ref[...] = (uc[:, HY_WIDTH:2 * HY_WIDTH] * uc[:, 2 * HY_WIDTH:]).T

    qkn = qk * lax.rsqrt(msq + EPS) * gqk_ref[...]

    cos = cos_ref[...]
    sin_signed = sin_ref[...]
    lane = lax.broadcasted_iota(jnp.int32, (1, LANES), 1)
    first_half = (lane % HEAD_DIM) < (HEAD_DIM // 2)

    def rope(xg):
        partner = jnp.where(first_half, pltpu.roll(xg, LANES - HEAD_DIM // 2, 1),
                            pltpu.roll(xg, HEAD_DIM // 2, 1))
        return xg * cos + partner * sin_signed

    for j in range(Q_COLS // LANES):
        r = rope(qkn[:, j * LANES:(j + 1) * LANES])
        q_ref[j * LANES:(j + 1) * LANES, :] = r.T.astype(BF16)
    r = rope(qkn[:, Q_COLS:QK_COLS]).astype(BF16)
    k_ref[0] = r[:, :HEAD_DIM]
    k_ref[1] = r[:, HEAD_DIM:]
    vt = qkv[:, QK_COLS:].T.astype(BF16)
    ones = jnp.ones((V_EXT - HEAD_DIM, tm), BF16)
    for g in range(N_KV_HEADS):
        v_ref[g * V_EXT:g * V_EXT + HEAD_DIM, :] = vt[g * HEAD_DIM:(g + 1) * HEAD_DIM, :]
        v_ref[g * V_EXT + HEAD_DIM:(g + 1) * V_EXT, :] = ones


def _inproj(x2, g, w_hy, w_qkv, gqk, mavg, inv_lane, isrow_lane, conv_w, conv_b, seq_len):
    n = x2.shape[0]
    tm = TOKEN_TILE
    halo_per_tile = tm // HALO
    last_halo = n // HALO - 1
    tiles_per_seq = seq_len // tm
    cos, sin_signed = _rope_tables(seq_len, inv_lane, isrow_lane)
    return pl.pallas_call(
        functools.partial(_inproj_kernel, seq_len=seq_len),
        grid=(n // tm,),
        in_specs=[
            pl.BlockSpec((tm, D_MODEL), lambda i: (i, 0)),
            pl.BlockSpec((HALO, D_MODEL), lambda i: (jnp.maximum(i * halo_per_tile - 1, 0), 0)),
            pl.BlockSpec((HALO, D_MODEL),
                         lambda i: (jnp.minimum((i + 1) * halo_per_tile, last_halo), 0)),
            _const_spec((1, D_MODEL)),
            _const_spec((D_MODEL, HY_COLS)),
            _const_spec((D_MODEL, QK_COLS + KV_COLS)),
            _const_spec((1, QK_COLS)),
            _const_spec((QK_COLS, QK_COLS)),
            pl.BlockSpec((tm, LANES), lambda i: (i % tiles_per_seq, 0)),
            pl.BlockSpec((tm, LANES), lambda i: (i % tiles_per_seq, 0)),
            _const_spec((SHORT_CONV, HY_COLS)),
            _const_spec((1, HY_COLS)),
        ],
        out_specs=[
            pl.BlockSpec((HY_WIDTH, tm), lambda i: (0, i)),
            pl.BlockSpec((tm, HY_WIDTH), lambda i: (i, 0)),
            pl.BlockSpec((Q_COLS, tm), lambda i: (0, i)),
            pl.BlockSpec((N_KV_HEADS, tm, HEAD_DIM), lambda i: (0, i, 0)),
            pl.BlockSpec((N_KV_HEADS * V_EXT, tm), lambda i: (0, i)),
        ],
        out_shape=[
            jax.ShapeDtypeStruct((HY_WIDTH, n), F32),
            jax.ShapeDtypeStruct((n, HY_WIDTH), F32),
            jax.ShapeDtypeStruct((Q_COLS, n), BF16),
            jax.ShapeDtypeStruct((N_KV_HEADS, n, HEAD_DIM), BF16),
            jax.ShapeDtypeStruct((N_KV_HEADS * V_EXT, n), BF16),
        ],
        scratch_shapes=[pltpu.VMEM((tm + 2 * HALO, HY_COLS), F32)],
        compiler_params=_params("parallel"),
        name="inproj",
    )(x2, x2, x2, g, w_hy, w_qkv, gqk, mavg, cos, sin_signed, conv_w, conv_b)


def _filter_kernel(w1t_ref, w1c_ref, w1s_ref, b1_ref, w2_ref, b2_ref, fr_ref, bands_ref,
                   w3_ref, decay_ref, k_ref, ksum_ref, *, seq_len):
    tl = k_ref.shape[1]
    hp = lax.Precision.HIGHEST
    p = pl.program_id(0) * tl + lax.broadcasted_iota(jnp.int32, (1, tl), 1)
    n = jnp.abs(p - seq_len).astype(F32)
    t = n * (1.0 / (seq_len - 1))
    w = n * (2.0 * math.pi / seq_len)
    ang = bands_ref[...] * w
    pre = (w1t_ref[...] * t
           + jnp.dot(w1c_ref[...], jnp.cos(ang), precision=hp, preferred_element_type=F32)
           - jnp.dot(w1s_ref[...], jnp.sin(ang), precision=hp, preferred_element_type=F32)
           + b1_ref[...])
    fr = fr_ref[...]
    h = jnp.sin(fr * pre)
    h = jnp.sin(fr * (jnp.dot(w2_ref[...], h, precision=hp, preferred_element_type=F32)
                      + b2_ref[...]))
    h = jnp.dot(w3_ref[...], h, precision=hp, preferred_element_type=F32)
    h = h * jnp.exp(-t * jnp.abs(decay_ref[...]))
    h = jnp.where(p == 0, 0.0, h)
    k_ref[...] = h

    @pl.when(pl.program_id(0) == 0)
    def _():
        ksum_ref[...] = jnp.zeros_like(ksum_ref)

    ksum_ref[...] += jnp.broadcast_to(jnp.sum(jnp.abs(h), axis=1, keepdims=True), ksum_ref.shape)


def _hyena_filter(seq_len, w1, b1, w2, b2, w3, freq, decay):
    tl = FILTER_TILE
    n_tiles = 2 * seq_len // tl
    half = seq_len // tl
    col = lambda a: a.reshape(-1, 1)
    bands = jnp.linspace(1e-4, FILTER_BANDS - 1, FILTER_BANDS, dtype=F32).reshape(-1, 1)
    w1t = w1.T
    args = (w1t[:, :1], w1t[:, 1:1 + FILTER_BANDS], w1t[:, 1 + FILTER_BANDS:], col(b1),
            w2.T, col(b2), col(freq), jnp.asarray(bands),
            w3.T, decay.reshape(-1, 1))
    dir_map = lambda i: (jnp.where(i >= half, 0, 1), 0)
    in_specs = [_const_spec(a.shape) for a in args[:8]]
    in_specs += [pl.BlockSpec((HY_WIDTH, FILTER_HIDDEN), dir_map),
                 pl.BlockSpec((HY_WIDTH, 1), dir_map)]
    return pl.pallas_call(
        functools.partial(_filter_kernel, seq_len=seq_len),
        grid=(n_tiles,),
        in_specs=in_specs,
        out_specs=[pl.BlockSpec((HY_WIDTH, tl), lambda i: (0, i)),
                   pl.BlockSpec((HY_WIDTH, LANES), lambda i: (0, 0))],
        out_shape=[jax.ShapeDtypeStruct((HY_WIDTH, 2 * seq_len), F32),
                   jax.ShapeDtypeStruct((HY_WIDTH, LANES), F32)],
        compiler_params=_params("arbitrary"),
        name="hyena_filter",
    )(*args)


def _longconv_kernel(z_ref, k_ref, ksum_ref, d_ref, y_ref, zrows2, acc2, tsub2):
    ncb, bsz, seq_len = z_ref.shape
    p = CONV_BLOCK
    q = CONV_SUB
    nb = seq_len // p

    def channel(c, slot):
        zrows, acc, tsub = zrows2.at[slot], acc2.at[slot], tsub2.at[slot]
        zc = z_ref[c]
        for i in range(nb):
            zrows[i * bsz:(i + 1) * bsz, :] = zc[:, i * p:(i + 1) * p]
        acc[...] = jnp.zeros_like(acc)
        inv = 1.0 / ksum_ref[pl.ds(c, 1), :][:, :1]
        for lo in range(0, 2 * seq_len, SKEW_CHUNK):
            src_lo = max(lo - q, 0)
            kn = k_ref[pl.ds(c, 1), src_lo:lo + SKEW_CHUNK] * inv
            skew = pltpu.roll(jnp.broadcast_to(kn, (q, kn.shape[1])), 0, 1,
                              stride=1, stride_axis=0)
            tsub[:, lo:lo + SKEW_CHUNK] = skew[:, lo - src_lo:].astype(BF16)

        def sub_tile(m):
            return tsub[:, seq_len + m * q:seq_len + (m + 1) * q]

        for d in range(-(nb - 1), nb):
            t_mid = sub_tile(2 * d)
            w_d = jnp.concatenate(
                [jnp.concatenate([t_mid, sub_tile(2 * d + 1)], axis=1),
                 jnp.concatenate([sub_tile(2 * d - 1), t_mid], axis=1)], axis=0)
            j0, j1 = max(0, -d), min(nb, nb - d)
            lhs = zrows[j0 * bsz:j1 * bsz, :].astype(BF16)
            acc[(j0 + d) * bsz:(j1 + d) * bsz, :] += jnp.dot(lhs, w_d, preferred_element_type=F32)
        dc = d_ref[pl.ds(c, 1), :][:, :1]
        for i in range(nb):
            y_ref[c, :, i * p:(i + 1) * p] = (acc[i * bsz:(i + 1) * bsz, :]
                                              + zc[:, i * p:(i + 1) * p] * dc)

    def channel_pair(i, carry):
        channel(2 * i, 0)
        channel(2 * i + 1, 1)
        return carry

    lax.fori_loop(0, ncb // 2, channel_pair, 0, unroll=True)


def _longconv(zt3, kext, ksum, d_bias):
    c, bsz, seq_len = zt3.shape
    cb = CONV_CH
    nb = seq_len // CONV_BLOCK
    return pl.pallas_call(
        _longconv_kernel,
        grid=(c // cb,),
        in_specs=[pl.BlockSpec((cb, bsz, seq_len), lambda i: (i, 0, 0)),
                  pl.BlockSpec((cb, 2 * seq_len), lambda i: (i, 0)),
                  pl.BlockSpec((cb, LANES), lambda i: (i, 0)),
                  pl.BlockSpec((cb, LANES), lambda i: (i, 0))],
        out_specs=pl.BlockSpec((cb, bsz, seq_len), lambda i: (i, 0, 0)),
        out_shape=jax.ShapeDtypeStruct((c, bsz, seq_len), F32),
        scratch_shapes=[pltpu.VMEM((2, nb * bsz, CONV_BLOCK), F32),
                        pltpu.VMEM((2, nb * bsz, CONV_BLOCK), F32),
                        pltpu.VMEM((2, CONV_SUB, 2 * seq_len), BF16)],
        compiler_params=_params("parallel"),
        name="hyena_longconv",
    )(zt3, kext, ksum, d_bias)


def _attn_kernel(q_ref, k_ref, v_ref, o_ref, m_ref, acc_ref, *, kv_tile):
    seq_len = k_ref.shape[1]
    m_ref[...] = jnp.full(m_ref.shape, -jnp.inf, F32)
    acc_ref[...] = jnp.zeros(acc_ref.shape, F32)

    def step(c, carry):
        off = pl.multiple_of(c * kv_tile, kv_tile)
        kc = k_ref[0, pl.ds(off, kv_tile), :]
        vc = v_ref[:, pl.ds(off, kv_tile)]
        for h in range(GROUP):
            qh = q_ref[h * HEAD_DIM:(h + 1) * HEAD_DIM, :]
            s = jnp.dot(kc, qh, preferred_element_type=F32)
            m_old = m_ref[h]
            m_new = jnp.maximum(m_old, jnp.max(s, axis=0, keepdims=True))
            alpha = jnp.exp(m_old - m_new)
            p = jnp.exp(s - m_new).astype(BF16)
            acc_ref[h] = alpha * acc_ref[h] + jnp.dot(vc, p, preferred_element_type=F32)
            m_ref[h] = m_new
        return carry

    lax.fori_loop(0, seq_len // kv_tile, step, 0)
    for h in range(GROUP):
        a = acc_ref[h]
        o_ref[h * HEAD_DIM:(h + 1) * HEAD_DIM, :] = a[:HEAD_DIM] / a[HEAD_DIM:HEAD_DIM + 1]


def _attn_bounded_kernel(q_ref, k_ref, v_ref, o_ref, qcat_ref, acc_ref, *, kv_tile):
    seq_len = k_ref.shape[1]
    tq = q_ref.shape[1]
    acc_ref[...] = jnp.zeros(acc_ref.shape, F32)
    for h in range(GROUP):
        qcat_ref[:, h * tq:(h + 1) * tq] = q_ref[h * HEAD_DIM:(h + 1) * HEAD_DIM, :]

    def step(c, carry):
        off = pl.multiple_of(c * kv_tile, kv_tile)
        kc = k_ref[0, pl.ds(off, kv_tile), :]
        vc = v_ref[:, pl.ds(off, kv_tile)]
        p = jnp.exp(jnp.dot(kc, qcat_ref[...], preferred_element_type=F32)).astype(BF16)
        acc_ref[...] += jnp.dot(vc, p, preferred_element_type=F32)
        return carry

    lax.fori_loop(0, seq_len // kv_tile, step, 0, unroll=True)
    for h in range(GROUP):
        a = acc_ref[:, h * tq:(h + 1) * tq]
        o_ref[h * HEAD_DIM:(h + 1) * HEAD_DIM, :] = a[:HEAD_DIM] / a[HEAD_DIM:HEAD_DIM + 1]


def _attention(qt, k, vt, bsz, seq_len, score_bound):
    n = bsz * seq_len
    tq = ATT_Q_TILE
    nq = seq_len // tq
    kv_tile = min(ATT_KV_TILE, seq_len)
    gw = GROUP * HEAD_DIM

    def call(body, scratch, name):
        return pl.pallas_call(
            functools.partial(body, kv_tile=kv_tile),
            grid=(bsz, N_KV_HEADS, nq),
            in_specs=[pl.BlockSpec((gw, tq), lambda b, h, i: (h, b * nq + i)),
                      pl.BlockSpec((1, seq_len, HEAD_DIM), lambda b, h, i: (h, b, 0)),
                      pl.BlockSpec((V_EXT, seq_len), lambda b, h, i: (h, b))],
            out_specs=pl.BlockSpec((gw, tq), lambda b, h, i: (h, b * nq + i)),
            out_shape=jax.ShapeDtypeStruct((ATT_WIDTH, n), F32),
            scratch_shapes=scratch,
            compiler_params=_params("parallel", "parallel", "parallel"),
            name=name,
        )

    acc = pltpu.VMEM((GROUP, V_EXT, tq), F32)
    bounded = call(_attn_bounded_kernel,
                   [pltpu.VMEM((HEAD_DIM, GROUP * tq), BF16), pltpu.VMEM((V_EXT, GROUP * tq), F32)],
                   "attention_bounded")
    online = call(_attn_kernel, [pltpu.VMEM((GROUP, 1, tq), F32), acc], "attention_online")
    return lax.cond(score_bound <= SCORE_BOUND_MAX, bounded, online, qt, k, vt)


def _rms(x, g):
    return x * lax.rsqrt(jnp.mean(x * x, axis=-1, keepdims=True) + EPS) * g


def _post_kernel(x_ref, yt_ref, x0_ref, ya_ref, ghy_ref, gatt_ref, wout_ref, gffn_ref,
                 wg_ref, wu_ref, wd_ref, gfin_ref, o_ref):
    hm = x_ref.shape[0] // 2
    halves = [slice(r * hm, (r + 1) * hm) for r in range(2)]

    def mixer_out(sl):
        y_hy = x0_ref[sl, :] * yt_ref[:, sl].T
        return jnp.concatenate([_rms(y_hy, ghy_ref[...]), _rms(ya_ref[:, sl].T, gatt_ref[...])],
                               axis=-1).astype(BF16)

    def swiglu(gate, up):
        return (gate * jax.nn.sigmoid(gate) * up).astype(BF16)

    def dot(a, w_ref):
        return jnp.dot(a, w_ref[...], preferred_element_type=F32)

    mixed = [mixer_out(sl) for sl in halves]
    x1 = [x_ref[sl, :] + dot(m, wout_ref) for sl, m in zip(halves, mixed)]
    h = [_rms(x, gffn_ref[...]).astype(BF16) for x in x1]
    gate_up = [(dot(hh, wg_ref), dot(hh, wu_ref)) for hh in h]
    ffn = [dot(swiglu(g, u), wd_ref) for g, u in gate_up]
    for sl, x, f in zip(halves, x1, ffn):
        o_ref[sl, :] = _rms(x + f, gfin_ref[...])


def _post(x2, yt, x0c, y_att, g_hy, g_att, w_out, g_ffn, w_gate, w_up, w_down, g_fin):
    n = x2.shape[0]
    tm = POST_TILE
    return pl.pallas_call(
        _post_kernel,
        grid=(n // tm,),
        in_specs=[
            pl.BlockSpec((tm, D_MODEL), lambda i: (i, 0)),
            pl.BlockSpec((HY_WIDTH, tm), lambda i: (0, i)),
            pl.BlockSpec((tm, HY_WIDTH), lambda i: (i, 0)),
            pl.BlockSpec((ATT_WIDTH, tm), lambda i: (0, i)),
            _const_spec((1, HY_WIDTH)),
            _const_spec((1, ATT_WIDTH)),
            _const_spec((D_MODEL, D_MODEL)),
            _const_spec((1, D_MODEL)),
            _const_spec((D_MODEL, D_FF)),
            _const_spec((D_MODEL, D_FF)),
            _const_spec((D_FF, D_MODEL)),
            _const_spec((1, D_MODEL)),
        ],
        out_specs=pl.BlockSpec((tm, D_MODEL), lambda i: (i, 0)),
        out_shape=jax.ShapeDtypeStruct((n, D_MODEL), F32),
        compiler_params=_params("parallel"),
        name="post",
    )(x2, yt, x0c, y_att, g_hy, g_att, w_out, g_ffn, w_gate, w_up, w_down, g_fin)


def _rope_lane_constants():
    lane = np.arange(LANES)
    pair = lane % (HEAD_DIM // 2)
    inv = ROPE_THETA ** (-jnp.arange(0, AXIS_DIM, 2, dtype=F32) / AXIS_DIM)
    isrow = (pair < AXIS_DIM // 2).astype(np.float32)
    return inv[pair % (AXIS_DIM // 2)].reshape(1, LANES), jnp.asarray(isrow.reshape(1, LANES))


def _prepare_weights(norm_mix_g, w_in, hy_conv_w, hy_conv_b, hy_d, q_norm_g, k_norm_g,
                     hy_out_g, att_out_g, w_out, norm_ffn_g, w_gate, w_up, w_down, final_norm_g):
    row = lambda a: a.reshape(1, -1).astype(F32)
    perm = np.concatenate([np.arange(0, HEAD_DIM, 2), np.arange(1, HEAD_DIM, 2)])
    qk_perm = np.concatenate([h * HEAD_DIM + perm for h in range(N_HEADS + N_KV_HEADS)])
    w_hy = w_in[:, :HY_COLS].astype(BF16)
    w_qk = w_in[:, HY_COLS:HY_COLS + QK_COLS][:, qk_perm]
    w_qkv = jnp.concatenate([w_qk, w_in[:, HY_COLS + QK_COLS:]], axis=1).astype(BF16)
    scale = HEAD_DIM ** -0.5
    gqk = jnp.concatenate([jnp.tile(q_norm_g[perm] * scale, N_HEADS),
                           jnp.tile(k_norm_g[perm], N_KV_HEADS)]).reshape(1, QK_COLS)
    head_id = np.arange(QK_COLS) // HEAD_DIM
    mavg = jnp.asarray((head_id[:, None] == head_id[None, :]).astype(np.float32) / HEAD_DIM,
                       dtype=BF16)
    inv_lane, isrow_lane = _rope_lane_constants()
    d_bias = jnp.broadcast_to(hy_d.reshape(HY_WIDTH, 1), (HY_WIDTH, LANES))
    score_bound = (HEAD_DIM ** 0.5) * jnp.max(jnp.abs(q_norm_g)) * jnp.max(jnp.abs(k_norm_g))
    return dict(
        score_bound=score_bound,
        inproj=(row(norm_mix_g), w_hy, w_qkv, gqk, mavg, inv_lane, isrow_lane, hy_conv_w,
                row(hy_conv_b)),
        d_bias=d_bias,
        post=(row(hy_out_g), row(att_out_g), w_out.astype(BF16), row(norm_ffn_g),
              w_gate.astype(BF16), w_up.astype(BF16), w_down.astype(BF16), row(final_norm_g)),
    )


def _trunk(x, wts, filt_params):
    bsz, seq_len, d_model = x.shape
    assert d_model == D_MODEL and x.dtype == F32
    assert seq_len % TOKEN_TILE == 0 and seq_len % ATT_Q_TILE == 0 and seq_len % GRID_W == 0
    assert seq_len % min(ATT_KV_TILE, seq_len) == 0 and 2 * seq_len % SKEW_CHUNK == 0
    assert seq_len % FILTER_TILE == 0 and seq_len % CONV_BLOCK == 0
    assert HY_WIDTH % CONV_CH == 0 and CONV_CH % 2 == 0
    assert bsz % SUBLANES == 0
    n = bsz * seq_len
    assert n % POST_TILE == 0
    x2 = x.reshape(n, D_MODEL)
    zt, x0c, q, k, v = _inproj(x2, *wts["inproj"], seq_len)
    kext, ksum = _hyena_filter(seq_len, *filt_params)
    yt = _longconv(zt.reshape(HY_WIDTH, bsz, seq_len), kext, ksum, wts["d_bias"])
    y_att = _attention(q, k, v, bsz, seq_len, wts["score_bound"])
    out = _post(x2, yt.reshape(HY_WIDTH, n), x0c, y_att, *wts["post"])
    return out.reshape(bsz, seq_len, D_MODEL)


def kernel(x_prompt, x_sample, norm_mix_g, w_in, hy_conv_w, hy_conv_b, hy_f_w1, hy_f_b1, hy_f_w2,
           hy_f_b2, hy_f_w3, hy_f_freq, hy_decay, hy_d, q_norm_g, k_norm_g, hy_out_g, att_out_g,
           w_out, norm_ffn_g, w_gate, w_up, w_down, final_norm_g):
    wts = _prepare_weights(norm_mix_g[0], w_in[0], hy_conv_w[0], hy_conv_b[0], hy_d[0],
                           q_norm_g[0], k_norm_g[0], hy_out_g[0], att_out_g[0], w_out[0],
                           norm_ffn_g[0], w_gate[0], w_up[0], w_down[0], final_norm_g)
    filt_params = (hy_f_w1[0], hy_f_b1[0], hy_f_w2[0], hy_f_b2[0], hy_f_w3[0], hy_f_freq[0],
                   hy_decay[0])
    return (_trunk(x_prompt, wts, filt_params), _trunk(x_sample, wts, filt_params))
```
